```python
import jax
import jax.numpy as jnp
from jax import lax
import numpy as np

D_MODEL = 1024
BATCH = 8
SEQ = 2048
DEPTH = 4
DEC_BATCH = 128
DEC_SEQ = 8
PAST_LEN = 8192
PAGE_SIZE = 128

N_MIXERS = 3
N_LAYERS_A = (DEPTH + 2) // 3
N_LAYERS_B = (DEPTH + 1) // 3
N_LAYERS_C = DEPTH // 3

N_HEADS = 12
HEAD_DIM = 64
N_KV_HEADS = 4
GQA = N_HEADS // N_KV_HEADS
HQ = N_HEADS * HEAD_DIM
KVW = N_KV_HEADS * HEAD_DIM
N_MEM = 256
N_MEM_HEADS = 4
MEM_HEAD_DIM = 64
MEMQ = N_MEM_HEADS * MEM_HEAD_DIM
MIX_WIDTH = HQ + MEMQ

Q_LORA = 768
KV_LORA = 256
QK_NOPE = 64
QK_ROPE = 32
V_HEAD = HEAD_DIM
ROPE_THETA = 10000.0

MOBA_BLOCK = 256
MOBA_TOPK = 3

CMP_BLOCK = 32
CMP_STRIDE = 16
CMP_HIDDEN = 64
SEL_BLOCK = 64
N_SEL = 16
WINDOW = 512

N_EXPERTS = 16
N_GROUPS = 4
EXPERTS_PER_GROUP = N_EXPERTS // N_GROUPS
TOP_K = 2
D_EXPERT = 512
MOE_ROWS = 128

IN_A = Q_LORA + KV_LORA + QK_ROPE + MEMQ
IN_B = HQ + 2 * KVW + MEMQ
IN_C = HQ + 6 * KVW + 3 * N_HEADS + MEMQ

ALPHA = (2 * DEPTH) ** 0.25
BETA = (8 * DEPTH) ** -0.25
EPS = 1e-5
QCHUNK = 128
NEG = -1e30
FORCE = 1e9
TINY = 1e-30
F32 = jnp.float32

kernel_name = 'hybrid_mla_moba_nsa_memx_moe_step'


def _alibi_slopes(n):
    def pow2(m):
        start = 2.0 ** (-8.0 / m)
        return [start ** (i + 1) for i in range(m)]
    if n & (n - 1) == 0:
        s = pow2(n)
    else:
        c = 2 ** int(np.floor(np.log2(n)))
        s = pow2(c) + pow2(2 * c)[0::2][:n - c]
    return jnp.asarray(np.array(s, np.float32))


def _layernorm(x, g, b):
    xf = x.astype(F32)
    xc = xf - xf.mean(-1, keepdims=True)
    var = (xc * xc).mean(-1, keepdims=True)
    return (xc * lax.rsqrt(var + EPS) * g + b).astype(x.dtype)


def _rmsnorm(x, g):
    xf = x.astype(F32)
    return (xf * lax.rsqrt((xf * xf).mean(-1, keepdims=True) + EPS) * g).astype(x.dtype)


def _rope(x, pos):
    half = x.shape[-1] // 2
    inv = ROPE_THETA ** (-jnp.arange(half, dtype=F32) / half)
    ang = pos.astype(F32)[:, None] * inv[None, :]
    cos, sin = jnp.cos(ang), jnp.sin(ang)
    if x.ndim == 4:
        cos, sin = cos[:, None], sin[:, None]
    x1, x2 = x[..., :half], x[..., half:]
    return jnp.concatenate([x1 * cos - x2 * sin, x2 * cos + x1 * sin], -1).astype(x.dtype)


def _masked_softmax(s, mask):
    s = jnp.where(mask, s, NEG)
    m = jnp.max(s, -1, keepdims=True)
    e = jnp.where(mask, jnp.exp(s - m), 0.0)
    return e / jnp.maximum(e.sum(-1, keepdims=True), TINY)


def _over_query_chunks(fn, qs, pos):
    t = pos.shape[0]
    if t <= QCHUNK or t % QCHUNK:
        return fn(qs, pos)
    n = t // QCHUNK
    split = lambda a: jnp.moveaxis(a.reshape((a.shape[0], n, QCHUNK) + a.shape[2:]), 1, 0)
    out = lax.map(lambda args: fn(args[0], args[1]), (tuple(split(a) for a in qs), pos.reshape(n, QCHUNK)))
    out = jnp.moveaxis(out, 0, 1)
    return out.reshape((out.shape[0], t) + out.shape[3:])


def _gather_pages(pool, j, page_table):
    rows = pool[page_table, j]
    return rows.reshape((rows.shape[0], rows.shape[1] * rows.shape[2]) + rows.shape[3:])


def _mla_mixer(z, pos, past, q_norm, kv_norm, w_uq, w_uk, w_uv):
    b, t, _ = z.shape
    cq = _rmsnorm(z[..., :Q_LORA], q_norm)
    ckv = _rmsnorm(z[..., Q_LORA:Q_LORA + KV_LORA], kv_norm)
    kpe = _rope(z[..., Q_LORA + KV_LORA:Q_LORA + KV_LORA + QK_ROPE], pos)
    q = jnp.einsum('btc,che->bthe', cq, w_uq)
    q_lat = jnp.einsum('bthn,chn->bthc', q[..., :QK_NOPE], w_uk)
    q_pe = _rope(q[..., QK_NOPE:], pos)
    if past is None:
        ckv_all, kpe_all = ckv, kpe
    else:
        ckv_all = jnp.concatenate([past[0], ckv], 1)
        kpe_all = jnp.concatenate([past[1], kpe], 1)
    k_pos = jnp.arange(ckv_all.shape[1])
    scale = (QK_NOPE + QK_ROPE) ** -0.5

    def block(qs, p):
        ql, qp = qs
        s = (jnp.einsum('bthc,bsc->bhts', ql, ckv_all, preferred_element_type=F32)
             + jnp.einsum('bthe,bse->bhts', qp, kpe_all, preferred_element_type=F32)) * scale
        pr = _masked_softmax(s, (k_pos[None, :] <= p[:, None])[None, None])
        return jnp.einsum('bhts,bsc->bthc', pr.astype(ckv_all.dtype), ckv_all)

    o_lat = _over_query_chunks(block, (q_lat, q_pe), pos)
    o = jnp.einsum('bthc,chv->bthv', o_lat, w_uv).reshape(b, t, N_HEADS * V_HEAD)
    return o, ckv, kpe


def _moba_mixer(z, pos, past, slopes):
    b, t, _ = z.shape
    q = z[..., :HQ].reshape(b, t, N_HEADS, HEAD_DIM)
    kv_new = z[..., HQ:HQ + 2 * KVW].reshape(b, t, 2, N_KV_HEADS, HEAD_DIM)
    kv_all = kv_new if past is None else jnp.concatenate([past, kv_new], 1)
    L = kv_all.shape[1]
    nblk = -(-L // MOBA_BLOCK)
    kv_pad = jnp.pad(kv_all, ((0, 0), (0, nblk * MOBA_BLOCK - L), (0, 0), (0, 0), (0, 0)))
    kvb = kv_pad.reshape(b, nblk, MOBA_BLOCK, 2, N_KV_HEADS, HEAD_DIM)
    k_mean = jnp.repeat(kvb[:, :, :, 0].astype(F32).mean(2), GQA, axis=2)
    k_sel = min(MOBA_TOPK, nblk)
    bi = jnp.arange(b)[:, None, None]
    hi = (jnp.arange(N_HEADS) // GQA)[None, None, :]
    scale = HEAD_DIM ** -0.5

    def chunk(qs, p):
        (qc,) = qs
        tc = p.shape[0]
        own = p // MOBA_BLOCK
        gate = jnp.einsum('bthd,bnhd->bthn', qc.astype(F32), k_mean)
        gate = jnp.where((jnp.arange(nblk)[None] < own[:, None])[None, :, None], gate, NEG)
        _, top = lax.top_k(gate, k_sel)
        blocks = [top[..., r] for r in range(k_sel)] + [jnp.broadcast_to(own[None, :, None], (b, tc, N_HEADS))]
        valid = [(own > r) for r in range(k_sel)] + [jnp.ones_like(own, dtype=bool)]
        scores, masks = [], []
        for blk, ok in zip(blocks, valid):
            kg = kvb[bi, blk, :, 0, hi]
            d = p[None, :, None, None] - (blk[..., None] * MOBA_BLOCK + jnp.arange(MOBA_BLOCK))
            s = jnp.einsum('bthd,bthsd->bths', qc, kg, preferred_element_type=F32) * scale
            scores.append(s - slopes[None, None, :, None] * d.astype(F32))
            masks.append((d >= 0) & ok[None, :, None, None])
        pr = _masked_softmax(jnp.concatenate(scores, -1), jnp.concatenate(masks, -1))
        out = None
        for r, blk in enumerate(blocks):
            vg = kvb[bi, blk, :, 1, hi]
            term = jnp.einsum('bths,bthsd->bthd', pr[..., r * MOBA_BLOCK:(r + 1) * MOBA_BLOCK].astype(vg.dtype), vg)
            out = term if out is None else out + term
        return out

    o = _over_query_chunks(chunk, (q,), pos).astype(z.dtype).reshape(b, t, HQ)
    return o, kv_new


def _compress(k, pe, w1, b1, w2, b2):
    b, L = k.shape[:2]
    n_sub = -(-L // CMP_STRIDE)
    k = jnp.pad(k, ((0, 0), (0, n_sub * CMP_STRIDE - L), (0, 0), (0, 0)))
    sub = k.reshape(b, n_sub, CMP_STRIDE, N_KV_HEADS, HEAD_DIM)
    w1r = w1.reshape(CMP_BLOCK, HEAD_DIM, CMP_HIDDEN)
    first = jnp.einsum('bnpkd,pdh->bnkh', sub, w1r[:CMP_STRIDE])
    second = jnp.einsum('bnpkd,pdh->bnkh', sub, w1r[CMP_STRIDE:])
    hid = jax.nn.gelu(first[:, :-1] + second[:, 1:] + (jnp.einsum('pd,pdh->h', pe, w1r) + b1))
    return jnp.einsum('bnkh,hd->bnkd', hid, w2) + b2


def _nsa_mixer(z, pos, q_start, past, b_gate, cmp_pe, phi_w1, phi_b1, phi_w2, phi_b2, slopes):
    b, t, _ = z.shape
    q = z[..., :HQ].reshape(b, t, N_KV_HEADS, GQA, HEAD_DIM)
    cmp_new = z[..., HQ:HQ + 2 * KVW].reshape(b, t, 2, N_KV_HEADS, HEAD_DIM)
    sel_new = z[..., HQ + 2 * KVW:HQ + 4 * KVW].reshape(b, t, 2, N_KV_HEADS, HEAD_DIM)
    win_new = z[..., HQ + 4 * KVW:HQ + 6 * KVW].reshape(b, t, 2, N_KV_HEADS, HEAD_DIM)
    gates = jax.nn.sigmoid((z[..., HQ + 6 * KVW:] + b_gate).astype(F32)).reshape(b, t, 3, N_KV_HEADS, GQA)
    if past is None:
        cmp_all, sel_all = cmp_new, sel_new
        win_seq = jnp.concatenate([jnp.zeros((b, WINDOW) + win_new.shape[2:], win_new.dtype), win_new], 1)
        win_pos0 = -WINDOW
        win_src = win_new
    else:
        cmp_all = jnp.concatenate([past[0], cmp_new], 1)
        sel_all = jnp.concatenate([past[1], sel_new], 1)
        win_seq = jnp.concatenate([past[2], win_new], 1)
        win_pos0 = q_start - past[2].shape[1]
        win_src = win_seq
    keep = min(WINDOW, q_start + t)
    win_state = win_src[:, win_src.shape[1] - keep:]

    k_c = _compress(cmp_all[:, :, 0], cmp_pe[0], phi_w1[0], phi_b1[0], phi_w2[0], phi_b2[0])
    v_c = _compress(cmp_all[:, :, 1], cmp_pe[1], phi_w1[1], phi_b1[1], phi_w2[1], phi_b2[1])
    n_cmp = k_c.shape[1]
    c_end = jnp.arange(n_cmp) * CMP_STRIDE + CMP_BLOCK - 1
    L = sel_all.shape[1]
    n_slc = -(-L // SEL_BLOCK)
    ci = jnp.arange(n_cmp)[:, None] * CMP_STRIDE
    sj = jnp.arange(n_slc)[None, :] * SEL_BLOCK
    cover = ((ci < sj + SEL_BLOCK) & (ci + CMP_BLOCK > sj)).astype(F32)
    sel_pad = jnp.pad(sel_all, ((0, 0), (0, n_slc * SEL_BLOCK - L), (0, 0), (0, 0), (0, 0)))
    sel_b = sel_pad.reshape(b, n_slc, SEL_BLOCK, 2, N_KV_HEADS, HEAD_DIM)
    n_top = min(N_SEL, n_slc)
    sl = slopes.reshape(N_KV_HEADS, GQA)
    bi = jnp.arange(b)[:, None, None, None]
    ki = jnp.arange(N_KV_HEADS)[None, None, :, None]
    scale = HEAD_DIM ** -0.5
    lw = win_seq.shape[1]

    def chunk(qs, p):
        qc, gc = qs
        tc = p.shape[0]
        d_c = p[:, None] - c_end[None, :]
        s = jnp.einsum('btkgd,bnkd->btkgn', qc, k_c, preferred_element_type=F32) * scale
        s = s - sl[None, None, :, :, None] * d_c[None, :, None, None, :].astype(F32)
        p_cmp = _masked_softmax(s, (d_c >= 0)[None, :, None, None, :])
        o_cmp = jnp.einsum('btkgn,bnkd->btkgd', p_cmp.astype(v_c.dtype), v_c)
        imp = jnp.einsum('btkgn,nj->btkj', p_cmp, cover)
        cur = p // SEL_BLOCK
        jj = jnp.arange(n_slc)[None]
        forced = (jj == 0) | (jj == cur[:, None]) | (jj == cur[:, None] - 1)
        imp = jnp.where((jj > cur[:, None])[None, :, None], NEG, jnp.where(forced[None, :, None], FORCE, imp))
        _, sel = lax.top_k(imp, n_top)
        d_s = p[None, :, None, None, None] - (sel[..., None] * SEL_BLOCK + jnp.arange(SEL_BLOCK))
        kg = sel_b[bi, sel, :, 0, ki]
        s = jnp.einsum('btkgd,btkjsd->btkgjs', qc, kg, preferred_element_type=F32) * scale
        s = s - sl[None, None, :, :, None, None] * d_s[:, :, :, None].astype(F32)
        s = s.reshape(b, tc, N_KV_HEADS, GQA, n_top * SEL_BLOCK)
        m_s = (d_s >= 0).reshape(b, tc, N_KV_HEADS, 1, n_top * SEL_BLOCK)
        p_sel = _masked_softmax(s, m_s).reshape(b, tc, N_KV_HEADS, GQA, n_top, SEL_BLOCK)
        vg = sel_b[bi, sel, :, 1, ki]
        o_sel = jnp.einsum('btkgjs,btkjsd->btkgd', p_sel.astype(vg.dtype), vg)
        wlen = min(tc + WINDOW, lw)
        start = jnp.clip(p[0] - WINDOW - win_pos0, 0, lw - wlen)
        kw = lax.dynamic_slice_in_dim(win_seq, start, wlen, axis=1)
        d_w = p[:, None] - (win_pos0 + start + jnp.arange(wlen))[None, :]
        m_w = (d_w >= 0) & (d_w <= WINDOW) & (d_w <= p[:, None])
        s = jnp.einsum('btkgd,bskd->btkgs', qc, kw[:, :, 0], preferred_element_type=F32) * scale
        s = s - sl[None, None, :, :, None] * d_w[None, :, None, None, :].astype(F32)
        p_w = _masked_softmax(s, m_w[None, :, None, None, :])
        o_win = jnp.einsum('btkgs,bskd->btkgd', p_w.astype(kw.dtype), kw[:, :, 1])
        return (gc[:, :, 0, :, :, None] * o_cmp + gc[:, :, 1, :, :, None] * o_sel
                + gc[:, :, 2, :, :, None] * o_win)

    o = _over_query_chunks(chunk, (q, gates), pos).astype(z.dtype).reshape(b, t, HQ)
    return o, cmp_new, sel_new, win_state


def _memory_attn(qm, mkv):
    b, t, _ = qm.shape
    q = qm.reshape(b, t, N_MEM_HEADS, MEM_HEAD_DIM)
    s = jnp.einsum('bthd,bmhd->bhtm', q, mkv[:, :, 0], preferred_element_type=F32) * MEM_HEAD_DIM ** -0.5
    pr = jax.nn.softmax(s, axis=-1)
    return jnp.einsum('bhtm,bmhd->bthd', pr.astype(mkv.dtype), mkv[:, :, 1]).reshape(b, t, MEMQ)


def _expert_ffn(xf, expert, gate, w_g, w_u, w_d):
    n_assign = expert.shape[0]
    n_blocks = -(-n_assign // MOE_ROWS) + N_EXPERTS
    onehot = (expert[:, None] == jnp.arange(N_EXPERTS)[None, :]).astype(jnp.int32)
    counts = onehot.sum(0)
    padded = (counts + MOE_ROWS - 1) // MOE_ROWS * MOE_ROWS
    ends = jnp.cumsum(padded)
    starts = ends - padded
    rank = (jnp.cumsum(onehot, 0) * onehot).sum(-1) - 1
    dest = starts[expert] + rank
    buf = jnp.zeros((n_blocks * MOE_ROWS, xf.shape[1]), xf.dtype).at[dest].set(xf[jnp.arange(n_assign) // TOP_K])
    blk_expert = jnp.minimum((jnp.arange(n_blocks)[:, None] * MOE_ROWS >= ends[None, :]).sum(-1), N_EXPERTS - 1)

    def ffn(args):
        xb, e = args
        h = jax.nn.silu(xb @ w_g[e]) * (xb @ w_u[e])
        return h @ w_d[e]

    out = lax.map(ffn, (buf.reshape(n_blocks, MOE_ROWS, -1), blk_expert)).reshape(n_blocks * MOE_ROWS, -1)
    return (out[dest] * gate[:, None]).reshape(-1, TOP_K, out.shape[-1]).sum(1)


def _moe(x, w_router, router_bias, w_g, w_u, w_d):
    b, t, d = x.shape
    xf = x.reshape(b * t, d)
    scores = jax.nn.sigmoid(jnp.einsum('nd,de->ne', xf, w_router, preferred_element_type=F32))
    biased = (scores + router_bias).reshape(-1, N_GROUPS, EXPERTS_PER_GROUP)
    grp = jnp.argmax(lax.top_k(biased, TOP_K)[0].sum(-1), axis=-1)
    in_grp = jnp.take_along_axis(biased, grp[:, None, None], axis=1)[:, 0]
    _, local = lax.top_k(in_grp, TOP_K)
    expert = grp[:, None] * EXPERTS_PER_GROUP + local
    w = jnp.take_along_axis(scores, expert, axis=1)
    w = w / w.sum(-1, keepdims=True)
    y = _expert_ffn(xf, expert.reshape(-1), w.reshape(-1), w_g, w_u, w_d)
    return y.reshape(b, t, d).astype(x.dtype)


def setup_inputs(seed: int = 0) -> dict:
    key = jax.random.key(seed)
    ks = jax.random.split(key, 40)

    def nrm(i, shape, scale=1.0):
        return jax.random.normal(ks[i], shape, F32) * scale

    n_pages = PAST_LEN // PAGE_SIZE
    n_used = DEC_BATCH * n_pages
    n_pool = n_used + max(1, n_used // 4)
    page_table = jax.random.permutation(ks[0], n_pool)[:n_used].reshape(DEC_BATCH, n_pages).astype(jnp.int32)
    win_keep = min(WINDOW, PAST_LEN)
    kvs = (2, N_KV_HEADS, HEAD_DIM)
    return {
        'x_prompt': nrm(1, (BATCH, SEQ, D_MODEL)),
        'x_sample': nrm(2, (DEC_BATCH, DEC_SEQ, D_MODEL)),
        'cache_mla_ckv': nrm(3, (n_pool, N_LAYERS_A, PAGE_SIZE, KV_LORA)),
        'cache_mla_kpe': nrm(4, (n_pool, N_LAYERS_A, PAGE_SIZE, QK_ROPE)),
        'cache_moba_kv': nrm(5, (n_pool, N_LAYERS_B, PAGE_SIZE) + kvs),
        'cache_nsa_cmp_kv': nrm(6, (n_pool, N_LAYERS_C, PAGE_SIZE) + kvs),
        'cache_nsa_sel_kv': nrm(7, (n_pool, N_LAYERS_C, PAGE_SIZE) + kvs),
        'state_nsa_win_kv': nrm(8, (N_LAYERS_C, DEC_BATCH, win_keep) + kvs),
        'cache_mem_kv': nrm(9, (DEPTH, DEC_BATCH, N_MEM, 2, N_MEM_HEADS, MEM_HEAD_DIM)),
        'page_table': page_table,
        'mem_prompt': nrm(10, (BATCH, N_MEM, D_MODEL)),
        'w_in_a': nrm(11, (N_LAYERS_A, D_MODEL, IN_A), D_MODEL ** -0.5),
        'q_norm_a': 1.0 + nrm(12, (N_LAYERS_A, Q_LORA), 0.02),
        'kv_norm_a': 1.0 + nrm(13, (N_LAYERS_A, KV_LORA), 0.02),
        'w_uq_a': nrm(14, (N_LAYERS_A, Q_LORA, N_HEADS, QK_NOPE + QK_ROPE), Q_LORA ** -0.5),
        'w_uk_a': nrm(15, (N_LAYERS_A, KV_LORA, N_HEADS, QK_NOPE), KV_LORA ** -0.5),
        'w_uv_a': nrm(16, (N_LAYERS_A, KV_LORA, N_HEADS, V_HEAD), KV_LORA ** -0.5),
        'w_in_b': nrm(17, (N_LAYERS_B, D_MODEL, IN_B), D_MODEL ** -0.5),
        'w_in_c': nrm(18, (N_LAYERS_C, D_MODEL, IN_C), D_MODEL ** -0.5),
        'b_gate_c': nrm(19, (N_LAYERS_C, 3 * N_HEADS), 0.1),
        'cmp_pe_c': nrm(20, (N_LAYERS_C, 2, CMP_BLOCK, HEAD_DIM), 0.1),
        'phi_w1_c': nrm(21, (N_LAYERS_C, 2, CMP_BLOCK * HEAD_DIM, CMP_HIDDEN), (CMP_BLOCK * HEAD_DIM) ** -0.5),
        'phi_b1_c': nrm(22, (N_LAYERS_C, 2, CMP_HIDDEN), 0.01),
        'phi_w2_c': nrm(23, (N_LAYERS_C, 2, CMP_HIDDEN, HEAD_DIM), CMP_HIDDEN ** -0.5),
        'phi_b2_c': nrm(24, (N_LAYERS_C, 2, HEAD_DIM), 0.01),
        'w_mem_kv': nrm(25, (DEPTH, D_MODEL, 2 * MEMQ), D_MODEL ** -0.5),
        'w_out': nrm(26, (DEPTH, MIX_WIDTH, D_MODEL), BETA * MIX_WIDTH ** -0.5),
        'ln1_g': 1.0 + nrm(27, (DEPTH, D_MODEL), 0.02),
        'ln1_b': nrm(28, (DEPTH, D_MODEL), 0.01),
        'ln2_g': 1.0 + nrm(29, (DEPTH, D_MODEL), 0.02),
        'ln2_b': nrm(30, (DEPTH, D_MODEL), 0.01),
        'w_router': nrm(31, (D_MODEL, N_EXPERTS), D_MODEL ** -0.5),
        'router_bias': nrm(32, (N_EXPERTS,), 0.01),
        'w_e_gate': nrm(33, (DEPTH, N_EXPERTS, D_MODEL, D_EXPERT), D_MODEL ** -0.5),
        'w_e_up': nrm(34, (DEPTH, N_EXPERTS, D_MODEL, D_EXPERT), D_MODEL ** -0.5),
        'w_e_down': nrm(35, (DEPTH, N_EXPERTS, D_EXPERT, D_MODEL), BETA * D_EXPERT ** -0.5),
    }


def reference(x_prompt, x_sample, cache_mla_ckv, cache_mla_kpe, cache_moba_kv, cache_nsa_cmp_kv,
              cache_nsa_sel_kv, state_nsa_win_kv, cache_mem_kv, page_table, mem_prompt,
              w_in_a, q_norm_a, kv_norm_a, w_uq_a, w_uk_a, w_uv_a, w_in_b, w_in_c, b_gate_c,
              cmp_pe_c, phi_w1_c, phi_b1_c, phi_w2_c, phi_b2_c, w_mem_kv, w_out,
              ln1_g, ln1_b, ln2_g, ln2_b, w_router, router_bias, w_e_gate, w_e_up, w_e_down):
    slopes = _alibi_slopes(N_HEADS)

    def trunk(x, q_start, sample):
        b, t, _ = x.shape
        pos = q_start + jnp.arange(t, dtype=jnp.int32)
        ckv_rows, kpe_rows, moba_rows, cmp_rows, sel_rows, win_states, mem_kvs = [], [], [], [], [], [], []
        for li in range(DEPTH):
            kind, j = li % N_MIXERS, li // N_MIXERS
            z = jnp.einsum('btd,de->bte', x, (w_in_a, w_in_b, w_in_c)[kind][j])
            z_mix, q_mem = z[..., :-MEMQ], z[..., -MEMQ:]
            if kind == 0:
                past = ((_gather_pages(cache_mla_ckv, j, page_table), _gather_pages(cache_mla_kpe, j, page_table))
                        if sample else None)
                o_mix, ckv, kpe = _mla_mixer(z_mix, pos, past, q_norm_a[j], kv_norm_a[j], w_uq_a[j], w_uk_a[j], w_uv_a[j])
                ckv_rows.append(ckv)
                kpe_rows.append(kpe)
            elif kind == 1:
                past = _gather_pages(cache_moba_kv, j, page_table) if sample else None
                o_mix, kv = _moba_mixer(z_mix, pos, past, slopes)
                moba_rows.append(kv)
            else:
                past = ((_gather_pages(cache_nsa_cmp_kv, j, page_table), _gather_pages(cache_nsa_sel_kv, j, page_table),
                         state_nsa_win_kv[j]) if sample else None)
                o_mix, ckv_c, sel_c, win_c = _nsa_mixer(z_mix, pos, q_start, past, b_gate_c[j], cmp_pe_c[j], phi_w1_c[j],
                                                       phi_b1_c[j], phi_w2_c[j], phi_b2_c[j], slopes)
                cmp_rows.append(ckv_c)
                sel_rows.append(sel_c)
                win_states.append(win_c)
            if sample:
                mkv = cache_mem_kv[li]
            else:
                mkv = jnp.einsum('bmd,de->bme', mem_prompt, w_mem_kv[li]).reshape(b, N_MEM, 2, N_MEM_HEADS, MEM_HEAD_DIM)
                mem_kvs.append(mkv)
            o = jnp.einsum('bte,ed->btd', jnp.concatenate([o_mix, _memory_attn(q_mem, mkv)], -1), w_out[li])
            x = _layernorm(ALPHA * x + o, ln1_g[li], ln1_b[li])
            x = _layernorm(ALPHA * x + _moe(x, w_router, router_bias, w_e_gate[li], w_e_up[li], w_e_down[li]),
                           ln2_g[li], ln2_b[li])
        rows = (jnp.stack(ckv_rows, 1), jnp.stack(kpe_rows, 1), jnp.stack(moba_rows, 1),
                jnp.stack(cmp_rows, 1), jnp.stack(sel_rows, 1), jnp.stack(win_states, 0))
        return x, rows, (jnp.stack(mem_kvs, 0) if mem_kvs else None)

    y_prompt, (p_ckv, p_kpe, p_moba, p_cmp, p_sel, p_win), p_mem = trunk(x_prompt, 0, False)
    y_sample, (s_ckv, s_kpe, s_moba, s_cmp, s_sel, s_win), _ = trunk(x_sample, PAST_LEN, True)
    return (y_prompt, y_sample, p_ckv, p_kpe, p_moba, p_cmp, p_sel, p_win, p_mem,
            s_ckv, s_kpe, s_moba, s_cmp, s_sel, s_win)
```

```python
import functools
import math

import numpy as np
import jax
import jax.numpy as jnp
from jax import lax
from jax.experimental import pallas as pl
from jax.experimental.pallas import tpu as pltpu

F32 = jnp.float32
BF16 = jnp.bfloat16
I32 = jnp.int32

D_MODEL = 1024
DEPTH = 4
PAGE_SIZE = 128
N_HEADS = 12
HEAD_DIM = 64
N_KV_HEADS = 4
GQA = N_HEADS // N_KV_HEADS
HQ = N_HEADS * HEAD_DIM
KVW = N_KV_HEADS * HEAD_DIM
N_MEM = 256
N_MEM_HEADS = 4
MEMQ = N_MEM_HEADS * HEAD_DIM
Q_LORA = 768
KV_LORA = 256
QK_NOPE = 64
QK_ROPE = 32
ROPE_THETA = 10000.0
MOBA_BLOCK = 256
MOBA_TOPK = 3
CMP_BLOCK = 32
CMP_STRIDE = 16
CMP_HIDDEN = 64
SEL_BLOCK = 64
N_SEL = 16
WINDOW = 512
N_EXPERTS = 16
N_GROUPS = 4
EXPERTS_PER_GROUP = N_EXPERTS // N_GROUPS
TOP_K = 2
D_EXPERT = 512
ALPHA = (2 * DEPTH) ** 0.25
EPS = 1e-5
NEG = -1e30
FORCE = 1e9
TINY = 1e-30

LANE = 128
SUBLANE = 8
VMEM_LIMIT = 56 * 1024 * 1024
MOE_TILE = 256
HIGHEST = lax.Precision.HIGHEST


def _cparams(sem):
    return pltpu.CompilerParams(dimension_semantics=sem, vmem_limit_bytes=VMEM_LIMIT)


def _alibi_slopes(n):
    def pow2(m):
        start = 2.0 ** (-8.0 / m)
        return [start ** (i + 1) for i in range(m)]
    if n & (n - 1) == 0:
        s = pow2(n)
    else:
        c = 2 ** int(np.floor(np.log2(n)))
        s = pow2(c) + pow2(2 * c)[0::2][:n - c]
    return np.array(s, np.float32)


def _nt(a, b, precision=None):
    return lax.dot_general(a, b, (((1,), (1,)), ((), ())), preferred_element_type=F32, precision=precision)


def _nn(a, b, precision=None):
    return jnp.dot(a, b, preferred_element_type=F32, precision=precision)


def _round_up(x, m):
    return -(-x // m) * m


def _row_call(body, n_rows, tm, ins, outs, name):
    assert n_rows % tm == 0
    n_tiles = n_rows // tm
    in_specs, arrays = [], []
    for arr, kind in ins:
        arrays.append(arr)
        if kind == 'row':
            in_specs.append(pl.BlockSpec((tm, arr.shape[1]), lambda i: (i, 0)))
        elif kind == 'rowT':
            in_specs.append(pl.BlockSpec((arr.shape[0], tm), lambda i: (0, i)))
        elif kind == 'period':
            per = arr.shape[0] // tm
            assert arr.shape[0] % tm == 0
            in_specs.append(pl.BlockSpec((tm, arr.shape[1]), lambda i, per=per: (i % per, 0)))
        else:
            nd = arr.ndim
            in_specs.append(pl.BlockSpec(arr.shape, lambda i, nd=nd: (0,) * nd))
    out_shapes, out_specs = [], []
    for width, dtype, transposed in outs:
        if transposed:
            out_shapes.append(jax.ShapeDtypeStruct((width, n_rows), dtype))
            out_specs.append(pl.BlockSpec((width, tm), lambda i: (0, i)))
        else:
            out_shapes.append(jax.ShapeDtypeStruct((n_rows, width), dtype))
            out_specs.append(pl.BlockSpec((tm, width), lambda i: (i, 0)))
    return pl.pallas_call(
        body, out_shape=out_shapes, grid=(n_tiles,), in_specs=in_specs, out_specs=out_specs,
        compiler_params=_cparams(("parallel",)), name=name)(*arrays)


def _rmsnorm(x, g):
    return x * lax.rsqrt(jnp.mean(x * x, axis=-1, keepdims=True) + EPS) * g


def _layernorm(x, g, b):
    xc = x - jnp.mean(x, axis=-1, keepdims=True)
    var = jnp.mean(xc * xc, axis=-1, keepdims=True)
    return xc * lax.rsqrt(var + EPS) * g + b


def _tile_lanes(x, n):
    return jnp.concatenate([x] * n, axis=1) if n > 1 else x


def _mm_body(x_ref, w_ref, o_ref, acc_ref, *, nk):
    k = pl.program_id(2)

    @pl.when(k == 0)
    def _():
        acc_ref[...] = jnp.zeros_like(acc_ref)

    acc_ref[...] += _nn(x_ref[...].astype(BF16), w_ref[...].astype(BF16))

    @pl.when(k == nk - 1)
    def _():
        o_ref[...] = acc_ref[...].astype(o_ref.dtype)


def _mm(x, w, *, tm, tn, tk, out_dtype, name):
    m, kd = x.shape
    n = w.shape[1]
    assert m % tm == 0 and n % tn == 0 and kd % tk == 0
    nk = kd // tk
    return pl.pallas_call(
        functools.partial(_mm_body, nk=nk),
        out_shape=jax.ShapeDtypeStruct((m, n), out_dtype),
        grid=(m // tm, n // tn, nk),
        in_specs=[pl.BlockSpec((tm, tk), lambda i, j, k: (i, k)),
                  pl.BlockSpec((tk, tn), lambda i, j, k: (k, j))],
        out_specs=pl.BlockSpec((tm, tn), lambda i, j, k: (i, j)),
        scratch_shapes=[pltpu.VMEM((tm, tn), F32)],
        compiler_params=_cparams(("parallel", "parallel", "arbitrary")), name=name)(x, w)


def _stack_rows(q_ref, n_groups, dq, rp):
    pieces = [q_ref[:, g * dq:(g + 1) * dq] for g in range(n_groups)]
    rows = sum(p.shape[0] for p in pieces)
    if rows < rp:
        pieces.append(jnp.zeros((rp - rows, dq), pieces[0].dtype))
    return jnp.concatenate(pieces, axis=0) if len(pieces) > 1 else pieces[0]


def _unstack_rows(o, n_groups, tq):
    pieces = [o[g * tq:(g + 1) * tq, :] for g in range(n_groups)]
    return jnp.concatenate(pieces, axis=1) if n_groups > 1 else pieces[0]


def _flash_body(*refs, cfg):
    (n_groups, dq, tq, tk, dv, rp, nkt, v_mode, has_slopes, has_sel, has_gate, sel_bs, causal, window,
     kpos0, kstride, klo_fn, khi_fn) = cfg
    refs = list(refs)
    q_ref = refs.pop(0)
    k_ref = refs.pop(0)
    v_ref = refs.pop(0) if v_mode != 'k' else None
    qpos_ref = refs.pop(0)
    slope_ref = refs.pop(0) if has_slopes else None
    sel_ref = refs.pop(0) if has_sel else None
    gate_ref = refs.pop(0) if has_gate else None
    o_ref, m_sc, l_sc, acc_sc = refs
    qi = pl.program_id(2)
    ki = pl.program_id(3)

    @pl.when(ki == 0)
    def _():
        m_sc[...] = jnp.full_like(m_sc, NEG)
        l_sc[...] = jnp.zeros_like(l_sc)
        acc_sc[...] = jnp.zeros_like(acc_sc)

    @pl.when((ki >= klo_fn(qi)) & (ki <= khi_fn(qi)))
    def _():
        qs = _stack_rows(q_ref, n_groups, dq, rp).astype(BF16)
        kt = k_ref[...].astype(BF16)
        s = _nt(kt, qs)
        kidx = ki * tk + lax.broadcasted_iota(I32, (tk, 1), 0)
        d = qpos_ref[...] - (kpos0 + kstride * kidx)
        if has_slopes:
            s = s - slope_ref[...] * d.astype(F32)
        if has_sel:
            nb = tk // sel_bs
            rows = [sel_ref[pl.ds(ki * nb + j, 1), :] for j in range(nb)]
            if nb == 1:
                s = s + rows[0]
            else:
                s = s + jnp.concatenate([jnp.broadcast_to(r, (sel_bs, rp)) for r in rows], axis=0)
        if causal:
            mask = d >= 0
            if window is not None:
                mask = mask & (d <= window)
            s = jnp.where(mask, s, NEG)
        m_prev = m_sc[...]
        m_new = jnp.maximum(m_prev, jnp.max(s, axis=0, keepdims=True))
        alpha = jnp.exp(m_prev - m_new)
        p = jnp.where(s > 0.5 * NEG, jnp.exp(s - m_new), 0.0)
        l_sc[...] = alpha * l_sc[...] + jnp.sum(p, axis=0, keepdims=True)
        if v_mode == 'vT':
            vt = v_ref[...].astype(BF16)
        elif v_mode == 'v':
            vt = v_ref[...].astype(F32).T.astype(BF16)
        else:
            vt = k_ref[:, :dv].astype(F32).T.astype(BF16)
        acc_sc[...] = alpha * acc_sc[...] + _nn(vt, p.astype(BF16))
        m_sc[...] = m_new

    @pl.when(ki == nkt - 1)
    def _():
        inv = jnp.where(m_sc[...] > 0.5 * NEG, 1.0 / jnp.maximum(l_sc[...], TINY), 0.0)
        if has_gate:
            inv = inv * gate_ref[...]
        o = (acc_sc[...] * inv).T
        o_ref[...] = _unstack_rows(o, n_groups, tq).astype(o_ref.dtype)


def _flash(q, k, v, qpos, *, n_groups, dq, dk, dv, tq, tk, kblk, vblk=None, v_mode='vT', slopes=None,
           sel=None, sel_bs=None, gates=None, causal=True, window=None, kpos0=0, kstride=1,
           klo_fn=None, khi_fn=None, out_dtype=BF16, name='flash'):
    nb, t_q, wq = q.shape
    t_k = k.shape[1]
    n_g = wq // (n_groups * dq)
    nqt, nkt = t_q // tq, t_k // tk
    rp = qpos.shape[-1]
    assert t_q % tq == 0 and t_k % tk == 0 and rp >= n_groups * tq and rp % LANE == 0
    if klo_fn is None:
        klo_fn = lambda qi: 0
    if khi_fn is None:
        khi_fn = lambda qi: nkt - 1

    def kclamp(qi, ki):
        return jnp.clip(ki, klo_fn(qi), khi_fn(qi))

    in_specs = [pl.BlockSpec((None, tq, n_groups * dq), lambda b, g, qi, ki: (b, qi, g)),
                pl.BlockSpec((None, tk, dk), lambda b, g, qi, ki: (b, kclamp(qi, ki), kblk(g)))]
    args = [q, k]
    if v_mode == 'vT':
        in_specs.append(pl.BlockSpec((None, dv, tk), lambda b, g, qi, ki: (b, vblk(g), kclamp(qi, ki))))
        args.append(v)
    elif v_mode == 'v':
        in_specs.append(pl.BlockSpec((None, tk, dv), lambda b, g, qi, ki: (b, kclamp(qi, ki), vblk(g))))
        args.append(v)
    in_specs.append(pl.BlockSpec((None, 1, rp), lambda b, g, qi, ki: (qi, 0, 0)))
    args.append(qpos)
    if slopes is not None:
        in_specs.append(pl.BlockSpec((None, 1, rp), lambda b, g, qi, ki: (g, 0, 0)))
        args.append(slopes)
    if sel is not None:
        nblk = sel.shape[3]
        in_specs.append(pl.BlockSpec((None, None, None, nblk, rp), lambda b, g, qi, ki: (b, g, qi, 0, 0)))
        args.append(sel)
    if gates is not None:
        in_specs.append(pl.BlockSpec((None, None, None, 1, rp), lambda b, g, qi, ki: (b, g, qi, 0, 0)))
        args.append(gates)
    cfg = (n_groups, dq, tq, tk, dv, rp, nkt, v_mode, slopes is not None, sel is not None, gates is not None,
           sel_bs, causal, window, kpos0, kstride, klo_fn, khi_fn)
    return pl.pallas_call(
        functools.partial(_flash_body, cfg=cfg),
        out_shape=jax.ShapeDtypeStruct((nb, t_q, n_g * n_groups * dv), out_dtype),
        grid=(nb, n_g, nqt, nkt), in_specs=in_specs,
        out_specs=pl.BlockSpec((None, tq, n_groups * dv), lambda b, g, qi, ki: (b, qi, g)),
        scratch_shapes=[pltpu.VMEM((1, rp), F32), pltpu.VMEM((1, rp), F32), pltpu.VMEM((dv, rp), F32)],
        compiler_params=_cparams(("parallel", "parallel", "parallel", "arbitrary")), name=name)(*args)


def _qpos_table(q_start, t_q, tq, n_groups, rp):
    nqt = t_q // tq
    base = q_start + np.arange(nqt)[:, None] * tq + np.tile(np.arange(tq), n_groups)[None, :]
    out = np.full((nqt, 1, rp), -1, np.int32)
    out[:, 0, :n_groups * tq] = base
    return jnp.asarray(out)


def _lane_table(per_group_vals, tq, rp):
    vals = np.asarray(per_group_vals, np.float32)
    n_g, n_groups = vals.shape
    out = np.zeros((n_g, 1, rp), np.float32)
    out[:, 0, :n_groups * tq] = np.repeat(vals, tq, axis=1)
    return jnp.asarray(out)


def _kmean_body(k_ref, o_ref):
    k = k_ref[...].astype(F32)
    w = k.shape[-1]
    o_ref[...] = jnp.sum(k.reshape(SUBLANE, MOBA_BLOCK, w), axis=1) * (1.0 / MOBA_BLOCK)


def _moba_kmean(k, n_mean):
    nb, _, w = k.shape
    assert n_mean % SUBLANE == 0
    return pl.pallas_call(
        _kmean_body, out_shape=jax.ShapeDtypeStruct((nb, n_mean, w), F32), grid=(nb, n_mean // SUBLANE),
        in_specs=[pl.BlockSpec((None, SUBLANE * MOBA_BLOCK, w), lambda b, c: (b, c, 0))],
        out_specs=pl.BlockSpec((None, SUBLANE, w), lambda b, c: (b, c, 0)),
        compiler_params=_cparams(("parallel", "parallel")), name='moba_kmean')(k)


def _rank_desc(vals, n_cand, row_iota):
    rank = jnp.zeros(vals.shape, I32)
    for i in range(n_cand):
        vi = vals[i:i + 1, :]
        rank = rank + ((vi > vals) | ((vi == vals) & (i < row_iota))).astype(I32)
    return rank


def _moba_gate_body(q_ref, km_ref, qpos_ref, o_ref, *, n_groups, dq, rp, n_cand):
    qs = _stack_rows(q_ref, n_groups, dq, rp)
    gate = _nt(km_ref[...], qs, precision=HIGHEST)
    nblk_pad = o_ref.shape[0]
    if nblk_pad > gate.shape[0]:
        gate = jnp.concatenate([gate, jnp.zeros((nblk_pad - gate.shape[0], rp), F32)], axis=0)
    own = qpos_ref[...] // MOBA_BLOCK
    n_iota = lax.broadcasted_iota(I32, gate.shape, 0)
    elig = n_iota < own
    g = jnp.where(elig, gate, NEG)
    rank = _rank_desc(g, n_cand, n_iota)
    selected = (elig & (rank < MOBA_TOPK)) | (n_iota == own)
    o_ref[...] = jnp.where(selected, 0.0, NEG)


def _moba_gate(qf, kmean, qpos, *, n_groups, dq, dk, tq, kblk, n_cand, nblk_pad):
    nb, t_q, wq = qf.shape
    n_g = wq // (n_groups * dq)
    nqt = t_q // tq
    rp = qpos.shape[-1]
    n_mean = kmean.shape[1]
    return pl.pallas_call(
        functools.partial(_moba_gate_body, n_groups=n_groups, dq=dq, rp=rp, n_cand=n_cand),
        out_shape=jax.ShapeDtypeStruct((nb, n_g, nqt, nblk_pad, rp), F32), grid=(nb, n_g, nqt),
        in_specs=[pl.BlockSpec((None, tq, n_groups * dq), lambda b, g, qi: (b, qi, g)),
                  pl.BlockSpec((None, n_mean, dk), lambda b, g, qi: (b, 0, kblk(g))),
                  pl.BlockSpec((None, 1, rp), lambda b, g, qi: (qi, 0, 0))],
        out_specs=pl.BlockSpec((None, None, None, nblk_pad, rp), lambda b, g, qi: (b, g, qi, 0, 0)),
        compiler_params=_cparams(("parallel", "parallel", "parallel")), name='moba_gate')(qf, kmean, qpos)


def _pad_head_cols(w, kv_of_head, slot=LANE):
    d = w.shape[0]
    n_h = len(kv_of_head)
    per = slot // HEAD_DIM
    out = jnp.zeros((d, n_h, per, HEAD_DIM), w.dtype)
    w3 = w.reshape(d, n_h, HEAD_DIM)
    for h, kv in enumerate(kv_of_head):
        out = out.at[:, h, kv % per, :].set(w3[:, h, :])
    return out.reshape(d, n_h * slot)


def _pad_head_rows(w, kv_of_head, slot=LANE):
    return _pad_head_cols(w.T, kv_of_head, slot).T


_KV_OF_HEAD = [h // GQA for h in range(N_HEADS)]
_KV_OF_MEM = list(range(N_MEM_HEADS))


def _causal_khi(tq, tk, q_start=0, kpos0=0):
    return lambda qi: (q_start - kpos0 + (qi + 1) * tq - 1) // tk


def _moba_prompt_attend(q, qf, kv, vt, slopes, *, tq=256):
    b, t, _ = q.shape
    tk = MOBA_BLOCK
    nblk = t // MOBA_BLOCK
    rp = GQA * tq
    qpos = _qpos_table(0, t, tq, GQA, rp)
    n_mean = _round_up(nblk, SUBLANE)
    kmean = _moba_kmean(kv, n_mean)
    pair = lambda g: g // 2
    sel = _moba_gate(qf, kmean, qpos, n_groups=GQA, dq=LANE, dk=LANE, tq=tq, kblk=pair, n_cand=nblk - 1,
                     nblk_pad=n_mean)
    slope_tab = _lane_table(slopes.reshape(N_KV_HEADS, GQA), tq, rp)
    return _flash(q, kv, vt, qpos, n_groups=GQA, dq=LANE, dk=LANE, dv=LANE, tq=tq, tk=tk, kblk=pair, vblk=pair,
                  v_mode='vT', slopes=slope_tab, sel=sel, sel_bs=MOBA_BLOCK, khi_fn=_causal_khi(tq, tk),
                  out_dtype=BF16, name='moba_flash')


def _compress_weight(phi_w1):
    w1r = phi_w1.reshape(2, 2, CMP_STRIDE, HEAD_DIM, CMP_HIDDEN)
    eye_c = jnp.eye(2, dtype=phi_w1.dtype)
    eye_k = jnp.eye(N_KV_HEADS, dtype=phi_w1.dtype)
    w = jnp.einsum('chpdx,ce,kl->pckdehlx', w1r, eye_c, eye_k)
    return w.reshape(CMP_STRIDE * 2 * KVW, 2 * 2 * N_KV_HEADS * CMP_HIDDEN)


def _block_diag4(w):
    return jnp.einsum('xy,kl->kxly', w, jnp.eye(N_KV_HEADS, dtype=w.dtype)).reshape(KVW, KVW)


def _compress2_body(fs_ref, pe_ref, w1_ref, b1_ref, w2k_ref, w2vt_ref, b2k_ref, b2v_ref, kc_ref, vct_ref):
    fs = fs_ref[...]
    n_sub = fs.shape[0]
    outs = []
    for c in range(2):
        first = fs[:, (2 * c) * KVW:(2 * c + 1) * KVW]
        second = fs[:, (2 * c + 1) * KVW:(2 * c + 2) * KVW]
        second = pltpu.roll(second, shift=n_sub - 1, axis=0)
        c1 = _nn(pe_ref[c], w1_ref[c], precision=HIGHEST)[0:1, :] + b1_ref[c]
        hid = jax.nn.gelu(first + second + _tile_lanes(c1, N_KV_HEADS))
        outs.append(hid.astype(BF16))
    kc_ref[...] = (_nn(outs[0], w2k_ref[...]) + b2k_ref[...]).astype(kc_ref.dtype)
    vct_ref[...] = (_nt(w2vt_ref[...], outs[1]) + b2v_ref[...]).astype(vct_ref.dtype)


def _nsa_compress(cmp_rows, cmp_pe, phi_w1, phi_b1, phi_w2, phi_b2):
    nb, length, _ = cmp_rows.shape
    n_sub = length // CMP_STRIDE
    stacked = cmp_rows.reshape(nb * n_sub, CMP_STRIDE * 2 * KVW)
    m = nb * n_sub
    tm = 512 if m % 512 == 0 else n_sub
    fs = _mm(stacked, _compress_weight(phi_w1).astype(BF16), tm=tm, tn=512, tk=2048, out_dtype=F32,
             name='nsa_compress_mm').reshape(nb, n_sub, 4 * KVW)
    pe = jnp.zeros((2, SUBLANE, CMP_BLOCK * HEAD_DIM), F32).at[:, 0, :].set(cmp_pe.reshape(2, -1))
    w2k = _block_diag4(phi_w2[0]).astype(BF16)
    w2vt = _block_diag4(phi_w2[1]).T.astype(BF16)
    b2k = jnp.tile(phi_b2[0], N_KV_HEADS).reshape(1, KVW)
    b2v = jnp.tile(phi_b2[1], N_KV_HEADS).reshape(KVW, 1)
    const = lambda a: pl.BlockSpec(a.shape, lambda b, nd=a.ndim: (0,) * nd)
    b1 = phi_b1.reshape(2, 1, CMP_HIDDEN)
    consts = [pe, phi_w1, b1, w2k, w2vt, b2k, b2v]
    return pl.pallas_call(
        _compress2_body,
        out_shape=[jax.ShapeDtypeStruct((nb, n_sub, KVW), BF16), jax.ShapeDtypeStruct((nb, KVW, n_sub), BF16)],
        grid=(nb,),
        in_specs=[pl.BlockSpec((None, n_sub, 4 * KVW), lambda b: (b, 0, 0))] + [const(a) for a in consts],
        out_specs=[pl.BlockSpec((None, n_sub, KVW), lambda b: (b, 0, 0)),
                   pl.BlockSpec((None, KVW, n_sub), lambda b: (b, 0, 0))],
        compiler_params=_cparams(("parallel",)), name='nsa_compress_mlp')(fs, *consts)


def _nsa_cmp_body(q_ref, kc_ref, vct_ref, qpos_ref, slope_ref, gate_ref, cover_ref, gsum_ref, o_ref, sel_ref, *,
                  n_groups, dq, tq, rp, n_cmp, n_slc, n_top, lane_sum):
    qs = _stack_rows(q_ref, n_groups, dq, rp).astype(BF16)
    s = _nt(kc_ref[...], qs)
    n_sub = s.shape[0]
    n_iota = lax.broadcasted_iota(I32, (n_sub, 1), 0)
    qpos = qpos_ref[...]
    d = qpos - (n_iota * CMP_STRIDE + (CMP_BLOCK - 1))
    s = s - slope_ref[...] * d.astype(F32)
    valid = (d >= 0) & (n_iota < n_cmp)
    s = jnp.where(valid, s, NEG)
    m = jnp.max(s, axis=0, keepdims=True)
    e = jnp.where(valid, jnp.exp(s - m), 0.0)
    p = e / jnp.maximum(jnp.sum(e, axis=0, keepdims=True), TINY)
    o = (_nn(vct_ref[...], p.astype(BF16)) * gate_ref[...]).T
    o_ref[...] = _unstack_rows(o, n_groups, tq).astype(o_ref.dtype)
    imp = _nn(cover_ref[...], p, precision=HIGHEST)
    if lane_sum:
        tot = imp[:, 0:tq]
        for g in range(1, n_groups):
            tot = tot + imp[:, g * tq:(g + 1) * tq]
        imp = _tile_lanes(tot, n_groups)
    else:
        imp = _nn(imp, gsum_ref[...], precision=HIGHEST)
    cur = qpos // SEL_BLOCK
    jj = lax.broadcasted_iota(I32, imp.shape, 0)
    forced = (jj == 0) | (jj == cur) | (jj == cur - 1)
    imp = jnp.where(jj > cur, NEG, jnp.where(forced, FORCE, imp))
    rank = _rank_desc(imp, n_slc, jj)
    sel_ref[...] = jnp.where((rank < n_top) & (jj < n_slc), 0.0, NEG)


def _nsa_cmp(q, kc, vct, qpos, slopes, gates, *, n_groups, dq, dk, dv, tq, kblk, n_cmp, n_slc, group_of_lane):
    nb, t_q, wq = q.shape
    n_g = wq // (n_groups * dq)
    nqt = t_q // tq
    rp = qpos.shape[-1]
    n_sub = kc.shape[1]
    n_slc_pad = _round_up(n_slc, SUBLANE)
    ci = np.arange(n_sub)[None, :] * CMP_STRIDE
    sj = np.arange(n_slc_pad)[:, None] * SEL_BLOCK
    cover_t = ((ci < sj + SEL_BLOCK) & (ci + CMP_BLOCK > sj) & (np.arange(n_sub)[None, :] < n_cmp)
               & (np.arange(n_slc_pad)[:, None] < n_slc)).astype(np.float32)
    lane_sum = tq % LANE == 0
    gl = np.asarray(group_of_lane)
    gsum = ((gl[:, None] == gl[None, :]) & (gl[:, None] >= 0)).astype(np.float32)
    n_top = min(N_SEL, n_slc)
    body = functools.partial(_nsa_cmp_body, n_groups=n_groups, dq=dq, tq=tq, rp=rp, n_cmp=n_cmp, n_slc=n_slc,
                             n_top=n_top, lane_sum=lane_sum)
    gidx = lambda b, g, qi: (b, g, qi, 0, 0)
    return pl.pallas_call(
        body,
        out_shape=[jax.ShapeDtypeStruct((nb, t_q, n_g * n_groups * dv), F32),
                   jax.ShapeDtypeStruct((nb, n_g, nqt, n_slc_pad, rp), F32)],
        grid=(nb, n_g, nqt),
        in_specs=[pl.BlockSpec((None, tq, n_groups * dq), lambda b, g, qi: (b, qi, g)),
                  pl.BlockSpec((None, n_sub, dk), lambda b, g, qi: (b, 0, kblk(g))),
                  pl.BlockSpec((None, dv, n_sub), lambda b, g, qi: (b, kblk(g), 0)),
                  pl.BlockSpec((None, 1, rp), lambda b, g, qi: (qi, 0, 0)),
                  pl.BlockSpec((None, 1, rp), lambda b, g, qi: (g, 0, 0)),
                  pl.BlockSpec((None, None, None, 1, rp), gidx),
                  pl.BlockSpec((n_slc_pad, n_sub), lambda b, g, qi: (0, 0)),
                  pl.BlockSpec((rp, rp), lambda b, g, qi: (0, 0))],
        out_specs=[pl.BlockSpec((None, tq, n_groups * dv), lambda b, g, qi: (b, qi, g)),
                   pl.BlockSpec((None, None, None, n_slc_pad, rp), gidx)],
        compiler_params=_cparams(("parallel", "parallel", "parallel")), name='nsa_cmp')(
            q, kc, vct, qpos, slopes, gates, jnp.asarray(cover_t), jnp.asarray(gsum))


def _gate_lanes(g_t, b, t, tq):
    nqt = t // tq
    g = g_t[:3 * N_HEADS].reshape(3, N_KV_HEADS, GQA, b, nqt, tq)
    return jnp.transpose(g, (0, 3, 1, 4, 2, 5)).reshape(3, b, N_KV_HEADS, nqt, 1, GQA * tq)


def _nsa_prompt_attend(q, cmp_kv, sel_kv, win_kv, vt_sel, vt_win, gates, slopes, phi, *, tq=256):
    b, t, _ = q.shape
    rp = GQA * tq
    qpos = _qpos_table(0, t, tq, GQA, rp)
    slope_tab = _lane_table(slopes.reshape(N_KV_HEADS, GQA), tq, rp)
    pair = lambda g: g // 2
    kc, vct = _nsa_compress(cmp_kv, *phi)
    n_cmp = t // CMP_STRIDE - 1
    n_slc = t // SEL_BLOCK
    group_of_lane = np.tile(np.arange(tq), GQA)
    o_cmp, sel = _nsa_cmp(q, kc, vct, qpos, slope_tab, gates[0], n_groups=GQA, dq=LANE, dk=LANE, dv=LANE, tq=tq,
                          kblk=pair, n_cmp=n_cmp, n_slc=n_slc, group_of_lane=group_of_lane)
    tk = 256
    o_sel = _flash(q, sel_kv, vt_sel, qpos, n_groups=GQA, dq=LANE, dk=LANE, dv=LANE, tq=tq, tk=tk, kblk=pair,
                   vblk=pair, slopes=slope_tab, sel=sel, sel_bs=SEL_BLOCK, gates=gates[1],
                   khi_fn=_causal_khi(tq, tk), out_dtype=F32, name='nsa_sel_flash')
    o_win = _flash(q, win_kv, vt_win, qpos, n_groups=GQA, dq=LANE, dk=LANE, dv=LANE, tq=tq, tk=tk, kblk=pair,
                   vblk=pair, slopes=slope_tab, gates=gates[2], window=WINDOW,
                   klo_fn=lambda qi: jnp.maximum(qi * tq - WINDOW, 0) // tk, khi_fn=_causal_khi(tq, tk),
                   out_dtype=F32, name='nsa_win_flash')
    return o_cmp, o_sel, o_win


def _inproj_body(*refs, plan, n_const):
    x_ref = refs[0]
    consts = refs[1:1 + n_const]
    outs = refs[1 + n_const:]
    xb = x_ref[...].astype(BF16)
    y = None
    for (kind, wi, bi, scale), o_ref in zip(plan, outs):
        if kind == 'mm':
            y = _nn(xb, consts[wi][...])
            if scale != 1.0:
                y = y * scale
        elif kind == 'mmT':
            y = _nt(consts[wi][...], xb)
        elif kind == 'gateT':
            y = jax.nn.sigmoid(_nt(consts[wi][...], xb) + consts[bi][...])
        o_ref[...] = y.astype(o_ref.dtype)


def _inproj(x, consts, plan, outs, tm, name):
    body = functools.partial(_inproj_body, plan=plan, n_const=len(consts))
    return _row_call(body, x.shape[0], tm, [(x, 'row')] + [(c, 'const') for c in consts], outs, name)


def _rope_tables(pos):
    half = QK_ROPE // 2
    inv = ROPE_THETA ** (-np.arange(half, dtype=np.float32) / half)
    ang = pos.astype(np.float32)[:, None] * inv[None, :]
    cos = np.concatenate([np.cos(ang), np.cos(ang)], -1).astype(np.float32)
    sin = np.concatenate([np.sin(ang), np.sin(ang)], -1).astype(np.float32)
    n = pos.shape[0]
    ctab = np.concatenate([np.ones((n, QK_NOPE), np.float32), cos, np.zeros((n, LANE - QK_NOPE - QK_ROPE), np.float32)], -1)
    stab = np.concatenate([np.zeros((n, QK_NOPE), np.float32), sin, np.zeros((n, LANE - QK_NOPE - QK_ROPE), np.float32)], -1)
    return jnp.asarray(cos), jnp.asarray(sin), jnp.asarray(ctab), jnp.asarray(stab)


def _rot_cols(w):
    half = QK_ROPE // 2
    return jnp.concatenate([-w[:, half:], w[:, :half]], axis=1)


def _inproj_mla_body(x_ref, cos_ref, sin_ref, ctab_ref, stab_ref, wcq, wckv, wpe, wper, wmem, qn, kvn, wuq, wuqr,
                     *rest, prompt, scale):
    if prompt:
        wuk, epe, wuvt, ckv_o, kpe_o, q_o, qmem_o, kh_o, vht_o = rest
    else:
        ckv_o, kpe_o, q_o, qmem_o = rest
    xb = x_ref[...].astype(BF16)
    cq = _rmsnorm(_nn(xb, wcq[...]), qn[...])
    ckv = _rmsnorm(_nn(xb, wckv[...]), kvn[...])
    kpe = _nn(xb, wpe[...]) * cos_ref[...] + _nn(xb, wper[...]) * sin_ref[...]
    ckv_o[...] = ckv
    kpe_o[...] = kpe
    cqb = cq.astype(BF16)
    q = _nn(cqb, wuq[...]) * _tile_lanes(ctab_ref[...], N_HEADS) + _nn(cqb, wuqr[...]) * _tile_lanes(stab_ref[...], N_HEADS)
    q_o[...] = (q * scale).astype(q_o.dtype)
    qmem_o[...] = (_nn(xb, wmem[...]) * (HEAD_DIM ** -0.5)).astype(qmem_o.dtype)
    if prompt:
        ckvb = ckv.astype(BF16)
        kh_o[...] = (_nn(ckvb, wuk[...]) + _nn(kpe.astype(BF16), epe[...])).astype(kh_o.dtype)
        vht_o[...] = _nt(wuvt[...], ckvb).astype(vht_o.dtype)


def _mla_weights(w_in, q_norm, kv_norm, w_uq, w_uk, w_uv, wmem_pad, prompt):
    e = QK_NOPE + QK_ROPE
    wcq = w_in[:, :Q_LORA].astype(BF16)
    wckv = w_in[:, Q_LORA:Q_LORA + KV_LORA].astype(BF16)
    wpe_f = w_in[:, Q_LORA + KV_LORA:Q_LORA + KV_LORA + QK_ROPE]
    pad = jnp.zeros((Q_LORA, N_HEADS, LANE - e), F32)
    wuq = jnp.concatenate([w_uq, pad], -1).reshape(Q_LORA, N_HEADS * LANE).astype(BF16)
    rot = jnp.concatenate([jnp.zeros((Q_LORA, N_HEADS, QK_NOPE), F32),
                           jax.vmap(_rot_cols, in_axes=1, out_axes=1)(w_uq[:, :, QK_NOPE:]), pad], -1)
    wuqr = rot.reshape(Q_LORA, N_HEADS * LANE).astype(BF16)
    consts = [wcq, wckv, wpe_f.astype(BF16), _rot_cols(wpe_f).astype(BF16), wmem_pad,
              q_norm.reshape(1, -1), kv_norm.reshape(1, -1), wuq, wuqr]
    if prompt:
        wuk = jnp.concatenate([w_uk, jnp.zeros((KV_LORA, N_HEADS, LANE - QK_NOPE), F32)], -1)
        epe = jnp.zeros((QK_ROPE, N_HEADS, LANE), F32).at[:, :, QK_NOPE:e].set(
            jnp.broadcast_to(jnp.eye(QK_ROPE, dtype=F32)[:, None, :], (QK_ROPE, N_HEADS, QK_ROPE)))
        wuvt = w_uv.reshape(KV_LORA, HQ).T
        consts += [wuk.reshape(KV_LORA, N_HEADS * LANE).astype(BF16), epe.reshape(QK_ROPE, N_HEADS * LANE).astype(BF16),
                   wuvt.astype(BF16)]
    return consts


def _inproj_mla(x, pos_rows, consts, tm, prompt, mem_w):
    cos, sin, ctab, stab = _rope_tables(pos_rows)
    n = x.shape[0]
    outs = [(KV_LORA, F32, False), (QK_ROPE, F32, False), (N_HEADS * LANE, BF16, False), (mem_w, BF16, False)]
    if prompt:
        outs += [(N_HEADS * LANE, BF16, False), (HQ, BF16, True)]
    body = functools.partial(_inproj_mla_body, prompt=prompt, scale=(QK_NOPE + QK_ROPE) ** -0.5)
    ins = [(x, 'row'), (cos, 'period'), (sin, 'period'), (ctab, 'period'), (stab, 'period')] + [(c, 'const') for c in consts]
    return _row_call(body, n, tm, ins, outs, 'inproj_mla')


def _outproj_body(*refs, set_sizes):
    n_o = sum(set_sizes)
    n_sets = len(set_sizes)
    x_ref = refs[0]
    o_refs = refs[1:1 + n_o]
    w_refs = refs[1 + n_o:1 + n_o + n_sets]
    g_ref, b_ref, wr_ref, rb_ref, x1_o, x1b_o, e_o, gw_o = refs[1 + n_o + n_sets:]
    acc = None
    idx = 0
    for si, n in enumerate(set_sizes):
        a = o_refs[idx][...]
        for j in range(1, n):
            a = a + o_refs[idx + j][...]
        idx += n
        term = _nn(a.astype(BF16), w_refs[si][...])
        acc = term if acc is None else acc + term
    x1 = _layernorm(ALPHA * x_ref[...] + acc, g_ref[...], b_ref[...])
    x1_o[...] = x1
    x1b_o[...] = x1.astype(BF16)
    scores = jax.nn.sigmoid(_nt(wr_ref[...], x1, precision=HIGHEST))
    biased = scores + rb_ref[...]
    rb = [biased[i:i + 1, :] for i in range(N_EXPERTS)]
    rs = [scores[i:i + 1, :] for i in range(N_EXPERTS)]
    gbest, gidx = None, None
    for g in range(N_GROUPS):
        v = rb[g * EXPERTS_PER_GROUP:(g + 1) * EXPERTS_PER_GROUP]
        top2 = None
        for i in range(EXPERTS_PER_GROUP):
            for j in range(i + 1, EXPERTS_PER_GROUP):
                pair = v[i] + v[j]
                top2 = pair if top2 is None else jnp.maximum(top2, pair)
        if g == 0:
            gbest, gidx = top2, jnp.zeros(top2.shape, I32)
        else:
            better = top2 > gbest
            gidx = jnp.where(better, g, gidx)
            gbest = jnp.where(better, top2, gbest)

    def member(rows, i):
        out = rows[(N_GROUPS - 1) * EXPERTS_PER_GROUP + i]
        for g in range(N_GROUPS - 2, -1, -1):
            out = jnp.where(gidx == g, rows[g * EXPERTS_PER_GROUP + i], out)
        return out

    vb = [member(rb, i) for i in range(EXPERTS_PER_GROUP)]
    vs = [member(rs, i) for i in range(EXPERTS_PER_GROUP)]
    b1, i1 = vb[0], jnp.zeros(gidx.shape, I32)
    for i in range(1, EXPERTS_PER_GROUP):
        better = vb[i] > b1
        i1 = jnp.where(better, i, i1)
        b1 = jnp.where(better, vb[i], b1)
    b2, i2 = jnp.full(b1.shape, -3e38, F32), jnp.zeros(gidx.shape, I32)
    for i in range(EXPERTS_PER_GROUP):
        ok = (i1 != i) & (vb[i] > b2)
        i2 = jnp.where(ok, i, i2)
        b2 = jnp.where(ok, vb[i], b2)

    def pick(rows, idx_):
        out = rows[EXPERTS_PER_GROUP - 1]
        for i in range(EXPERTS_PER_GROUP - 2, -1, -1):
            out = jnp.where(idx_ == i, rows[i], out)
        return out

    w1, w2 = pick(vs, i1), pick(vs, i2)
    den = w1 + w2
    e_o[...] = jnp.concatenate([gidx * EXPERTS_PER_GROUP + i1, gidx * EXPERTS_PER_GROUP + i2], axis=0)
    gw_o[...] = jnp.concatenate([w1 / den, w2 / den], axis=0)


def _outproj(x, o_sets, ln_g, ln_b, w_router, router_bias, tm):
    n = x.shape[0]
    ins = [(x, 'row')]
    for arrs, _ in o_sets:
        ins += [(a, 'row') for a in arrs]
    ins += [(w, 'const') for _, w in o_sets]
    ins += [(ln_g.reshape(1, -1), 'const'), (ln_b.reshape(1, -1), 'const'), (w_router.T, 'const'),
            (router_bias.reshape(-1, 1), 'const')]
    outs = [(D_MODEL, F32, False), (D_MODEL, BF16, False), (TOP_K, I32, True), (TOP_K, F32, True)]
    body = functools.partial(_outproj_body, set_sizes=tuple(len(a) for a, _ in o_sets))
    return _row_call(body, n, tm, ins, outs, 'outproj_ln_router')


def _expert_body(be_ref, nu_ref, x_ref, wg_ref, wu_ref, wd_ref, o_ref, wg_sc, wu_sc, wd_sc):
    i = pl.program_id(0)
    changed = (i == 0) | (be_ref[jnp.maximum(i - 1, 0)] != be_ref[i])

    @pl.when(changed)
    def _():
        wg_sc[...] = wg_ref[...].astype(BF16)
        wu_sc[...] = wu_ref[...].astype(BF16)
        wd_sc[...] = wd_ref[...].astype(BF16)

    @pl.when(i < nu_ref[0])
    def _():
        xb = x_ref[...]
        h = jax.nn.silu(_nn(xb, wg_sc[...])) * _nn(xb, wu_sc[...])
        o_ref[...] = _nn(h.astype(BF16), wd_sc[...])

    @pl.when(i >= nu_ref[0])
    def _():
        o_ref[...] = jnp.zeros_like(o_ref)


def _experts(xg, blk_expert, n_used, w_g, w_u, w_d):
    n_slots = xg.shape[0]
    n_blocks = n_slots // MOE_TILE
    grid_spec = pltpu.PrefetchScalarGridSpec(
        num_scalar_prefetch=2, grid=(n_blocks,),
        in_specs=[pl.BlockSpec((MOE_TILE, D_MODEL), lambda i, be, nu: (i, 0)),
                  pl.BlockSpec((None, D_MODEL, D_EXPERT), lambda i, be, nu: (be[i], 0, 0)),
                  pl.BlockSpec((None, D_MODEL, D_EXPERT), lambda i, be, nu: (be[i], 0, 0)),
                  pl.BlockSpec((None, D_EXPERT, D_MODEL), lambda i, be, nu: (be[i], 0, 0))],
        out_specs=pl.BlockSpec((MOE_TILE, D_MODEL), lambda i, be, nu: (i, 0)),
        scratch_shapes=[pltpu.VMEM((D_MODEL, D_EXPERT), BF16), pltpu.VMEM((D_MODEL, D_EXPERT), BF16),
                        pltpu.VMEM((D_EXPERT, D_MODEL), BF16)])
    return pl.pallas_call(
        _expert_body, out_shape=jax.ShapeDtypeStruct((n_slots, D_MODEL), F32), grid_spec=grid_spec,
        compiler_params=_cparams(("arbitrary",)), name='moe_experts')(blk_expert, n_used, xg, w_g, w_u, w_d)


def _ln2_body(x_ref, y0_ref, y1_ref, gw_ref, g_ref, b_ref, o_ref):
    gw = gw_ref[...]
    y = gw[:, 0:1] * y0_ref[...] + gw[:, 1:2] * y1_ref[...]
    o_ref[...] = _layernorm(ALPHA * x_ref[...] + y, g_ref[...], b_ref[...])


def _moe(x1, x1b, expert_t, gate_t, w_g, w_u, w_d, ln_g, ln_b, tm):
    n = x1.shape[0]
    n_assign = TOP_K * n
    e_flat = expert_t.reshape(-1)
    onehot = (e_flat[:, None] == jnp.arange(N_EXPERTS)[None, :]).astype(I32)
    counts = onehot.sum(0)
    padded = (counts + MOE_TILE - 1) // MOE_TILE * MOE_TILE
    ends = jnp.cumsum(padded)
    starts = ends - padded
    rank = (jnp.cumsum(onehot, 0) * onehot).sum(-1) - 1
    dest = starts[e_flat] + rank
    n_blocks = -(-n_assign // MOE_TILE) + N_EXPERTS
    n_slots = n_blocks * MOE_TILE
    slot_tok = jnp.zeros((n_slots,), I32).at[dest].set(jnp.arange(n_assign, dtype=I32) % n)
    blk_expert = jnp.minimum((jnp.arange(n_blocks)[:, None] * MOE_TILE >= ends[None, :]).sum(-1), N_EXPERTS - 1).astype(I32)
    n_used = (ends[-1] // MOE_TILE).astype(I32).reshape(1)
    xg = jnp.take(x1b, slot_tok, axis=0)
    out = _experts(xg, blk_expert, n_used, w_g, w_u, w_d)
    y = jnp.take(out, dest, axis=0).reshape(TOP_K, n, D_MODEL)
    ins = [(x1, 'row'), (y[0], 'row'), (y[1], 'row'), (gate_t.T, 'row'),
           (ln_g.reshape(1, -1), 'const'), (ln_b.reshape(1, -1), 'const')]
    return _row_call(_ln2_body, n, tm, ins, [(D_MODEL, F32, False)], 'moe_combine_ln')[0]


def _row_tile(n, cap=256):
    tm = cap
    while n % tm:
        tm //= 2
    assert tm >= SUBLANE
    return tm


def _feat_major(a_t, b, t):
    return jnp.transpose(a_t.reshape(a_t.shape[0], b, t), (1, 0, 2))


def _gather_pages(pool, j, page_table):
    rows = pool[page_table, j]
    return rows.reshape(rows.shape[0], rows.shape[1] * rows.shape[2], -1)


def _pad_rows(a, n):
    return jnp.pad(a, ((0, 0), (0, n - a.shape[1]), (0, 0)))


def _prompt_trunk(x3, mem_prompt, p, slopes):
    b, t, _ = x3.shape
    n = b * t
    x = x3.reshape(n, D_MODEL)
    tm = _row_tile(n)
    pair = lambda g: g // 2
    ckv_rows, kpe_rows, moba_rows, cmp_rows, sel_rows, win_states, mem_kvs = [], [], [], [], [], [], []
    mem2d = mem_prompt.reshape(b * N_MEM, D_MODEL)
    for li in range(DEPTH):
        kind, j = li % 3, li // 3
        w_in = (p['w_in_a'], p['w_in_b'], p['w_in_c'])[kind][j]
        wmem_pad = _pad_head_cols(w_in[:, -MEMQ:], _KV_OF_MEM).astype(BF16)
        w_out = p['w_out'][li]
        wm = p['w_mem_kv'][li]
        mkv, vtm = _inproj(mem2d, [wm.astype(BF16), wm[:, MEMQ:].T.astype(BF16)],
                           [('mm', 0, None, 1.0), ('mmT', 1, None, 1.0)],
                           [(2 * MEMQ, F32, False), (MEMQ, BF16, True)], _row_tile(b * N_MEM), 'mem_kv')
        mem_kvs.append(mkv.reshape(b, N_MEM, 2, N_MEM_HEADS, HEAD_DIM))
        if kind == 0:
            consts = _mla_weights(w_in, p['q_norm_a'][j], p['kv_norm_a'][j], p['w_uq_a'][j], p['w_uk_a'][j],
                                  p['w_uv_a'][j], wmem_pad, True)
            ckv, kpe, q, qmem, kh, vht = _inproj_mla(x, np.arange(t), consts, tm, True, N_MEM_HEADS * LANE)
            ckv_rows.append(ckv.reshape(b, t, KV_LORA))
            kpe_rows.append(kpe.reshape(b, t, QK_ROPE))
            tq, tk = min(256, t), min(512, t)
            qpos = _qpos_table(0, t, tq, 1, tq)
            o_mix = _flash(q.reshape(b, t, -1), kh.reshape(b, t, -1), _feat_major(vht, b, t), qpos, n_groups=1,
                           dq=LANE, dk=LANE, dv=LANE, tq=tq, tk=tk, kblk=lambda g: g, vblk=pair,
                           khi_fn=_causal_khi(tq, tk), out_dtype=BF16, name='mla_flash')
            o_sets = [([o_mix.reshape(n, -1)], _pad_head_rows(w_out[:HQ], list(range(N_HEADS))).astype(BF16))]
        elif kind == 1:
            consts = [_pad_head_cols(w_in[:, :HQ], _KV_OF_HEAD).astype(BF16), w_in[:, HQ:HQ + 2 * KVW].astype(BF16),
                      w_in[:, HQ + KVW:HQ + 2 * KVW].T.astype(BF16), wmem_pad]
            plan = [('mm', 0, None, HEAD_DIM ** -0.5), ('prev', 0, None, 1.0), ('mm', 1, None, 1.0),
                    ('mmT', 2, None, 1.0), ('mm', 3, None, HEAD_DIM ** -0.5)]
            outs = [(N_HEADS * LANE, BF16, False), (N_HEADS * LANE, F32, False), (2 * KVW, F32, False),
                    (KVW, BF16, True), (N_MEM_HEADS * LANE, BF16, False)]
            q, qf, kv, vt, qmem = _inproj(x, consts, plan, outs, tm, 'inproj_moba')
            moba_rows.append(kv.reshape(b, t, 2, N_KV_HEADS, HEAD_DIM))
            o_mix = _moba_prompt_attend(q.reshape(b, t, -1), qf.reshape(b, t, -1), kv.reshape(b, t, -1),
                                        _feat_major(vt, b, t), slopes, tq=min(256, t))
            o_sets = [([o_mix.reshape(n, -1)], _pad_head_rows(w_out[:HQ], _KV_OF_HEAD).astype(BF16))]
        else:
            c0 = HQ
            wg_t = jnp.zeros((40, D_MODEL), F32).at[:3 * N_HEADS].set(w_in[:, c0 + 6 * KVW:c0 + 6 * KVW + 3 * N_HEADS].T)
            bg = jnp.zeros((40, 1), F32).at[:3 * N_HEADS, 0].set(p['b_gate_c'][j])
            consts = [_pad_head_cols(w_in[:, :HQ], _KV_OF_HEAD).astype(BF16),
                      w_in[:, c0:c0 + 2 * KVW].astype(BF16), w_in[:, c0 + 2 * KVW:c0 + 4 * KVW].astype(BF16),
                      w_in[:, c0 + 4 * KVW:c0 + 6 * KVW].astype(BF16),
                      w_in[:, c0 + 3 * KVW:c0 + 4 * KVW].T.astype(BF16), w_in[:, c0 + 5 * KVW:c0 + 6 * KVW].T.astype(BF16),
                      wg_t.astype(BF16), bg, wmem_pad]
            plan = [('mm', 0, None, HEAD_DIM ** -0.5), ('mm', 1, None, 1.0), ('mm', 2, None, 1.0), ('mm', 3, None, 1.0),
                    ('mmT', 4, None, 1.0), ('mmT', 5, None, 1.0), ('gateT', 6, 7, 1.0), ('mm', 8, None, HEAD_DIM ** -0.5)]
            outs = [(N_HEADS * LANE, BF16, False), (2 * KVW, F32, False), (2 * KVW, F32, False), (2 * KVW, F32, False),
                    (KVW, BF16, True), (KVW, BF16, True), (40, F32, True), (N_MEM_HEADS * LANE, BF16, False)]
            q, cmp_kv, sel_kv, win_kv, vts, vtw, g_t, qmem = _inproj(x, consts, plan, outs, tm, 'inproj_nsa')
            kvs = (2, N_KV_HEADS, HEAD_DIM)
            cmp_rows.append(cmp_kv.reshape((b, t) + kvs))
            sel_rows.append(sel_kv.reshape((b, t) + kvs))
            keep = min(WINDOW, t)
            win_states.append(win_kv.reshape((b, t) + kvs)[:, t - keep:])
            tq = min(256, t)
            phi = (p['cmp_pe_c'][j], p['phi_w1_c'][j], p['phi_b1_c'][j], p['phi_w2_c'][j], p['phi_b2_c'][j])
            o3 = _nsa_prompt_attend(q.reshape(b, t, -1), cmp_kv.reshape(b, t, -1), sel_kv.reshape(b, t, -1),
                                    win_kv.reshape(b, t, -1), _feat_major(vts, b, t), _feat_major(vtw, b, t),
                                    _gate_lanes(g_t, b, t, tq), slopes, phi, tq=tq)
            o_sets = [([o.reshape(n, -1) for o in o3], _pad_head_rows(w_out[:HQ], _KV_OF_HEAD).astype(BF16))]
        tqm = min(512, t)
        o_mem = _flash(qmem.reshape(b, t, -1), mkv.reshape(b, N_MEM, -1), _feat_major(vtm, b, N_MEM),
                       _qpos_table(0, t, tqm, 1, tqm), n_groups=1, dq=LANE, dk=LANE, dv=LANE, tq=tqm, tk=N_MEM,
                       kblk=pair, vblk=pair, causal=False, out_dtype=BF16, name='mem_flash')
        o_sets.append(([o_mem.reshape(n, -1)], _pad_head_rows(w_out[HQ:], _KV_OF_MEM).astype(BF16)))
        x1, x1b, e_t, g_w = _outproj(x, o_sets, p['ln1_g'][li], p['ln1_b'][li], p['w_router'], p['router_bias'], tm)
        x = _moe(x1, x1b, e_t, g_w, p['w_e_gate'][li], p['w_e_up'][li], p['w_e_down'][li],
                 p['ln2_g'][li], p['ln2_b'][li], tm)
    return (x.reshape(b, t, D_MODEL), jnp.stack(ckv_rows, 1), jnp.stack(kpe_rows, 1), jnp.stack(moba_rows, 1),
            jnp.stack(cmp_rows, 1), jnp.stack(sel_rows, 1), jnp.stack(win_states, 0), jnp.stack(mem_kvs, 0))


_SLOT_S = 4 * HEAD_DIM


def _sample_lane_gates(g_t, db, ts, rp):
    g = g_t[:3 * N_HEADS].reshape(3, N_HEADS, db, ts)
    g = jnp.transpose(g, (0, 2, 1, 3)).reshape(3, db, N_HEADS * ts)
    return jnp.pad(g, ((0, 0), (0, 0), (0, rp - N_HEADS * ts))).reshape(3, db, 1, 1, 1, rp)


def _sample_trunk(x3, p, caches, page_table, slopes):
    db, ts, _ = x3.shape
    n = db * ts
    past = page_table.shape[1] * PAGE_SIZE
    x = x3.reshape(n, D_MODEL)
    tm = _row_tile(n)
    rp = LANE
    assert N_HEADS * ts <= rp and ts % SUBLANE == 0
    zero = lambda g: 0
    one = lambda g: 1
    qpos = _qpos_table(past, ts, ts, N_HEADS, rp)
    qpos_mem = _qpos_table(past, ts, ts, N_MEM_HEADS, rp)
    slope_tab = _lane_table(slopes.reshape(1, N_HEADS), ts, rp)
    ckv_rows, kpe_rows, moba_rows, cmp_rows, sel_rows, win_states = [], [], [], [], [], []
    pos_rows = past + np.arange(max(tm, ts)) % ts
    kvs = (2, N_KV_HEADS, HEAD_DIM)
    for li in range(DEPTH):
        kind, j = li % 3, li // 3
        w_in = (p['w_in_a'], p['w_in_b'], p['w_in_c'])[kind][j]
        wmem_pad = _pad_head_cols(w_in[:, -MEMQ:], _KV_OF_MEM, _SLOT_S).astype(BF16)
        w_out = p['w_out'][li]
        if kind == 0:
            consts = _mla_weights(w_in, p['q_norm_a'][j], p['kv_norm_a'][j], p['w_uq_a'][j], p['w_uk_a'][j],
                                  p['w_uv_a'][j], wmem_pad, False)
            ckv, kpe, q, qmem = _inproj_mla(x, pos_rows, consts, tm, False, N_MEM_HEADS * _SLOT_S)
            ckv_rows.append(ckv.reshape(db, ts, KV_LORA))
            kpe_rows.append(kpe.reshape(db, ts, QK_ROPE))
            slot_a = KV_LORA + LANE
            w_abs = jnp.zeros((N_HEADS, LANE, N_HEADS, slot_a), F32)
            eye = jnp.eye(QK_ROPE, dtype=F32)
            for h in range(N_HEADS):
                w_abs = w_abs.at[h, :QK_NOPE, h, :KV_LORA].set(p['w_uk_a'][j][:, h, :].T)
                w_abs = w_abs.at[h, QK_NOPE:QK_NOPE + QK_ROPE, h, KV_LORA:KV_LORA + QK_ROPE].set(eye)
            q_abs = _mm(q, w_abs.reshape(N_HEADS * LANE, N_HEADS * slot_a).astype(BF16), tm=tm, tn=slot_a,
                        tk=N_HEADS * LANE, out_dtype=BF16, name='mla_absorb_q')
            tk = 2816
            lk = _round_up(past + ts, tk)
            k_cat = jnp.concatenate([
                jnp.concatenate([_gather_pages(caches['mla_ckv'], j, page_table), ckv.reshape(db, ts, -1)], 1),
                jnp.concatenate([_gather_pages(caches['mla_kpe'], j, page_table), kpe.reshape(db, ts, -1)], 1),
            ], -1).astype(BF16)
            k_cat = jnp.pad(k_cat, ((0, 0), (0, lk - past - ts), (0, slot_a - KV_LORA - QK_ROPE)))
            o_lat = _flash(q_abs.reshape(db, ts, -1), k_cat, None, qpos, n_groups=N_HEADS, dq=slot_a, dk=slot_a,
                           dv=KV_LORA, tq=ts, tk=tk, kblk=zero, v_mode='k', out_dtype=BF16, name='mla_decode')
            w_uv_bd = jnp.einsum('chv,hg->hcgv', p['w_uv_a'][j], jnp.eye(N_HEADS, dtype=F32)).reshape(
                N_HEADS * KV_LORA, HQ)
            o_mix = _mm(o_lat.reshape(n, -1), w_uv_bd.astype(BF16), tm=tm, tn=HQ, tk=N_HEADS * KV_LORA,
                        out_dtype=BF16, name='mla_value_up')
            o_sets = [([o_mix], w_out[:HQ].astype(BF16))]
        elif kind == 1:
            consts = [_pad_head_cols(w_in[:, :HQ], _KV_OF_HEAD, _SLOT_S).astype(BF16),
                      w_in[:, HQ:HQ + 2 * KVW].astype(BF16), wmem_pad]
            plan = [('mm', 0, None, HEAD_DIM ** -0.5), ('prev', 0, None, 1.0), ('mm', 1, None, 1.0),
                    ('mm', 2, None, HEAD_DIM ** -0.5)]
            outs = [(N_HEADS * _SLOT_S, BF16, False), (N_HEADS * _SLOT_S, F32, False), (2 * KVW, F32, False),
                    (N_MEM_HEADS * _SLOT_S, BF16, False)]
            q, qf, kv, qmem = _inproj(x, consts, plan, outs, tm, 'inproj_moba_s')
            moba_rows.append(kv.reshape((db, ts) + kvs))
            nblk = -(-(past + ts) // MOBA_BLOCK)
            tk = 11 * MOBA_BLOCK
            lk = _round_up(nblk * MOBA_BLOCK, tk)
            kv_all = _pad_rows(jnp.concatenate([_gather_pages(caches['moba_kv'], j, page_table),
                                                kv.reshape(db, ts, -1)], 1), lk)
            n_mean = past // MOBA_BLOCK
            kmean = _moba_kmean(kv_all, n_mean)
            sel = _moba_gate(qf.reshape(db, ts, -1), kmean, qpos, n_groups=N_HEADS, dq=_SLOT_S, dk=_SLOT_S, tq=ts,
                             kblk=zero, n_cand=n_mean, nblk_pad=_round_up(lk // MOBA_BLOCK, SUBLANE))
            o_mix = _flash(q.reshape(db, ts, -1), kv_all, kv_all, qpos, n_groups=N_HEADS, dq=_SLOT_S, dk=_SLOT_S,
                           dv=_SLOT_S, tq=ts, tk=tk, kblk=zero, vblk=one, v_mode='v', slopes=slope_tab, sel=sel,
                           sel_bs=MOBA_BLOCK, out_dtype=BF16, name='moba_decode')
            o_sets = [([o_mix.reshape(n, -1)], _pad_head_rows(w_out[:HQ], _KV_OF_HEAD, _SLOT_S).astype(BF16))]
        else:
            c0 = HQ
            wg_t = jnp.zeros((40, D_MODEL), F32).at[:3 * N_HEADS].set(w_in[:, c0 + 6 * KVW:c0 + 6 * KVW + 3 * N_HEADS].T)
            bg = jnp.zeros((40, 1), F32).at[:3 * N_HEADS, 0].set(p['b_gate_c'][j])
            consts = [_pad_head_cols(w_in[:, :HQ], _KV_OF_HEAD, _SLOT_S).astype(BF16),
                      w_in[:, c0:c0 + 2 * KVW].astype(BF16), w_in[:, c0 + 2 * KVW:c0 + 4 * KVW].astype(BF16),
                      w_in[:, c0 + 4 * KVW:c0 + 6 * KVW].astype(BF16), wg_t.astype(BF16), bg, wmem_pad]
            plan = [('mm', 0, None, HEAD_DIM ** -0.5), ('mm', 1, None, 1.0), ('mm', 2, None, 1.0), ('mm', 3, None, 1.0),
                    ('gateT', 4, 5, 1.0), ('mm', 6, None, HEAD_DIM ** -0.5)]
            outs = [(N_HEADS * _SLOT_S, BF16, False), (2 * KVW, F32, False), (2 * KVW, F32, False),
                    (2 * KVW, F32, False), (40, F32, True), (N_MEM_HEADS * _SLOT_S, BF16, False)]
            q, cmp_kv, sel_kv, win_kv, g_t, qmem = _inproj(x, consts, plan, outs, tm, 'inproj_nsa_s')
            cmp_rows.append(cmp_kv.reshape((db, ts) + kvs))
            sel_rows.append(sel_kv.reshape((db, ts) + kvs))
            win_seq = jnp.concatenate([caches['nsa_win'][j].reshape(db, -1, 2 * KVW), win_kv.reshape(db, ts, -1)], 1)
            keep = min(WINDOW, past + ts)
            win_states.append(win_seq[:, win_seq.shape[1] - keep:].reshape((db, keep) + kvs))
            gates = _sample_lane_gates(g_t, db, ts, rp)
            q3 = q.reshape(db, ts, -1)
            length = past + ts
            n_cmp = -(-length // CMP_STRIDE) - 1
            l_cmp = _round_up(_round_up(length, CMP_STRIDE) + CMP_STRIDE, SUBLANE * CMP_STRIDE)
            cmp_all = _pad_rows(jnp.concatenate([_gather_pages(caches['nsa_cmp'], j, page_table),
                                                 cmp_kv.reshape(db, ts, -1)], 1), l_cmp)
            phi = (p['cmp_pe_c'][j], p['phi_w1_c'][j], p['phi_b1_c'][j], p['phi_w2_c'][j], p['phi_b2_c'][j])
            kc, vct = _nsa_compress(cmp_all, *phi)
            n_slc = -(-length // SEL_BLOCK)
            group_of_lane = np.full((rp,), -1, np.int64)
            lanes = np.arange(N_HEADS * ts)
            group_of_lane[:N_HEADS * ts] = (lanes // ts // GQA) * ts + lanes % ts
            o_cmp, sel = _nsa_cmp(q3, kc, vct, qpos, slope_tab, gates[0], n_groups=N_HEADS, dq=_SLOT_S, dk=_SLOT_S,
                                  dv=_SLOT_S, tq=ts, kblk=zero, n_cmp=n_cmp, n_slc=n_slc, group_of_lane=group_of_lane)
            tk = n_slc * SEL_BLOCK // 3 if (n_slc % 3 == 0 and (n_slc // 3 * SEL_BLOCK) % SUBLANE == 0) else n_slc * SEL_BLOCK
            sel_all = _pad_rows(jnp.concatenate([_gather_pages(caches['nsa_sel'], j, page_table),
                                                 sel_kv.reshape(db, ts, -1)], 1), n_slc * SEL_BLOCK)
            o_sel = _flash(q3, sel_all, sel_all, qpos, n_groups=N_HEADS, dq=_SLOT_S, dk=_SLOT_S, dv=_SLOT_S, tq=ts,
                           tk=tk, kblk=zero, vblk=one, v_mode='v', slopes=slope_tab, sel=sel, sel_bs=SEL_BLOCK,
                           gates=gates[1], out_dtype=F32, name='nsa_sel_decode')
            lw = win_seq.shape[1]
            o_win = _flash(q3, win_seq, win_seq, qpos, n_groups=N_HEADS, dq=_SLOT_S, dk=_SLOT_S, dv=_SLOT_S, tq=ts,
                           tk=lw, kblk=zero, vblk=one, v_mode='v', slopes=slope_tab, gates=gates[2], window=WINDOW,
                           kpos0=past - (lw - ts), out_dtype=F32, name='nsa_win_decode')
            o_sets = [([o.reshape(n, -1) for o in (o_cmp, o_sel, o_win)],
                       _pad_head_rows(w_out[:HQ], _KV_OF_HEAD, _SLOT_S).astype(BF16))]
        mkv = caches['mem_kv'][li].reshape(db, N_MEM, 2 * MEMQ)
        o_mem = _flash(qmem.reshape(db, ts, -1), mkv, mkv, qpos_mem, n_groups=N_MEM_HEADS, dq=_SLOT_S, dk=_SLOT_S,
                       dv=_SLOT_S, tq=ts, tk=N_MEM, kblk=zero, vblk=one, v_mode='v', causal=False, out_dtype=BF16,
                       name='mem_decode')
        o_sets.append(([o_mem.reshape(n, -1)], _pad_head_rows(w_out[HQ:], _KV_OF_MEM, _SLOT_S).astype(BF16)))
        x1, x1b, e_t, g_w = _outproj(x, o_sets, p['ln1_g'][li], p['ln1_b'][li], p['w_router'], p['router_bias'], tm)
        x = _moe(x1, x1b, e_t, g_w, p['w_e_gate'][li], p['w_e_up'][li], p['w_e_down'][li],
                 p['ln2_g'][li], p['ln2_b'][li], tm)
    return (x.reshape(db, ts, D_MODEL), jnp.stack(ckv_rows, 1), jnp.stack(kpe_rows, 1), jnp.stack(moba_rows, 1),
            jnp.stack(cmp_rows, 1), jnp.stack(sel_rows, 1), jnp.stack(win_states, 0))


def kernel(x_prompt, x_sample, cache_mla_ckv, cache_mla_kpe, cache_moba_kv, cache_nsa_cmp_kv, cache_nsa_sel_kv,
           state_nsa_win_kv, cache_mem_kv, page_table, mem_prompt, w_in_a, q_norm_a, kv_norm_a, w_uq_a, w_uk_a,
           w_uv_a, w_in_b, w_in_c, b_gate_c, cmp_pe_c, phi_w1_c, phi_b1_c, phi_w2_c, phi_b2_c, w_mem_kv, w_out,
           ln1_g, ln1_b, ln2_g, ln2_b, w_router, router_bias, w_e_gate, w_e_up, w_e_down):
    p = dict(w_in_a=w_in_a, q_norm_a=q_norm_a, kv_norm_a=kv_norm_a, w_uq_a=w_uq_a, w_uk_a=w_uk_a, w_uv_a=w_uv_a,
             w_in_b=w_in_b, w_in_c=w_in_c, b_gate_c=b_gate_c, cmp_pe_c=cmp_pe_c, phi_w1_c=phi_w1_c,
             phi_b1_c=phi_b1_c, phi_w2_c=phi_w2_c, phi_b2_c=phi_b2_c, w_mem_kv=w_mem_kv, w_out=w_out, ln1_g=ln1_g,
             ln1_b=ln1_b, ln2_g=ln2_g, ln2_b=ln2_b, w_router=w_router, router_bias=router_bias, w_e_gate=w_e_gate,
             w_e_up=w_e_up, w_e_down=w_e_down)
    caches = dict(mla_ckv=cache_mla_ckv, mla_kpe=cache_mla_kpe, moba_kv=cache_moba_kv, nsa_cmp=cache_nsa_cmp_kv,
                  nsa_sel=cache_nsa_sel_kv, nsa_win=state_nsa_win_kv, mem_kv=cache_mem_kv)
    slopes = _alibi_slopes(N_HEADS)
    y_p, p_ckv, p_kpe, p_moba, p_cmp, p_sel, p_win, p_mem = _prompt_trunk(x_prompt, mem_prompt, p, slopes)
    y_s, s_ckv, s_kpe, s_moba, s_cmp, s_sel, s_win = _sample_trunk(x_sample, p, caches, page_table, slopes)
    return (y_p, y_s, p_ckv, p_kpe, p_moba, p_cmp, p_sel, p_win, p_mem,
            s_ckv, s_kpe, s_moba, s_cmp, s_sel, s_win)
```

```python
import functools
import math

import numpy as np
import jax
import jax.numpy as jnp
from jax import lax
from jax.experimental import pallas as pl
from jax.experimental.pallas import tpu as pltpu

F32 = jnp.float32
BF16 = jnp.bfloat16
I32 = jnp.int32

D_MODEL = 1024
DEPTH = 4
PAGE_SIZE = 128
N_HEADS = 12
HEAD_DIM = 64
N_KV_HEADS = 4
GQA = N_HEADS // N_KV_HEADS
HQ = N_HEADS * HEAD_DIM
KVW = N_KV_HEADS * HEAD_DIM
N_MEM = 256
N_MEM_HEADS = 4
MEMQ = N_MEM_HEADS * HEAD_DIM
Q_LORA = 768
KV_LORA = 256
QK_NOPE = 64
QK_ROPE = 32
ROPE_THETA = 10000.0
MOBA_BLOCK = 256
MOBA_TOPK = 3
CMP_BLOCK = 32
CMP_STRIDE = 16
CMP_HIDDEN = 64
SEL_BLOCK = 64
N_SEL = 16
WINDOW = 512
N_EXPERTS = 16
N_GROUPS = 4
EXPERTS_PER_GROUP = N_EXPERTS // N_GROUPS
TOP_K = 2
D_EXPERT = 512
ALPHA = (2 * DEPTH) ** 0.25
EPS = 1e-5
NEG = -1e30
FORCE = 1e9
TINY = 1e-30

LANE = 128
SUBLANE = 8
VMEM_LIMIT = 56 * 1024 * 1024
MOE_TILE = 256
HIGHEST = lax.Precision.HIGHEST


def _cparams(sem):
    return pltpu.CompilerParams(dimension_semantics=sem, vmem_limit_bytes=VMEM_LIMIT)


def _alibi_slopes(n):
    def pow2(m):
        start = 2.0 ** (-8.0 / m)
        return [start ** (i + 1) for i in range(m)]
    if n & (n - 1) == 0:
        s = pow2(n)
    else:
        c = 2 ** int(np.floor(np.log2(n)))
        s = pow2(c) + pow2(2 * c)[0::2][:n - c]
    return np.array(s, np.float32)


def _nt(a, b, precision=None):
    return lax.dot_general(a, b, (((1,), (1,)), ((), ())), preferred_element_type=F32, precision=precision)


def _nn(a, b, precision=None):
    return jnp.dot(a, b, preferred_element_type=F32, precision=precision)


def _round_up(x, m):
    return -(-x // m) * m


def _row_call(body, n_rows, tm, ins, outs, name):
    assert n_rows % tm == 0
    n_tiles = n_rows // tm
    in_specs, arrays = [], []
    for arr, kind in ins:
        arrays.append(arr)
        if kind == 'row':
            in_specs.append(pl.BlockSpec((tm, arr.shape[1]), lambda i: (i, 0)))
        elif kind == 'rowT':
            in_specs.append(pl.BlockSpec((arr.shape[0], tm), lambda i: (0, i)))
        elif kind == 'period':
            per = arr.shape[0] // tm
            assert arr.shape[0] % tm == 0
            in_specs.append(pl.BlockSpec((tm, arr.shape[1]), lambda i, per=per: (i % per, 0)))
        else:
            nd = arr.ndim
            in_specs.append(pl.BlockSpec(arr.shape, lambda i, nd=nd: (0,) * nd))
    out_shapes, out_specs = [], []
    for width, dtype, transposed in outs:
        if transposed:
            out_shapes.append(jax.ShapeDtypeStruct((width, n_rows), dtype))
            out_specs.append(pl.BlockSpec((width, tm), lambda i: (0, i)))
        else:
            out_shapes.append(jax.ShapeDtypeStruct((n_rows, width), dtype))
            out_specs.append(pl.BlockSpec((tm, width), lambda i: (i, 0)))
    return pl.pallas_call(
        body, out_shape=out_shapes, grid=(n_tiles,), in_specs=in_specs, out_specs=out_specs,
        compiler_params=_cparams(("parallel",)), name=name)(*arrays)


def _rmsnorm(x, g):
    return x * lax.rsqrt(jnp.mean(x * x, axis=-1, keepdims=True) + EPS) * g


def _layernorm(x, g, b):
    xc = x - jnp.mean(x, axis=-1, keepdims=True)
    var = jnp.mean(xc * xc, axis=-1, keepdims=True)
    return xc * lax.rsqrt(var + EPS) * g + b


def _tile_lanes(x, n):
    return jnp.concatenate([x] * n, axis=1) if n > 1 else x


def _mm_body(x_ref, w_ref, o_ref, acc_ref, *, nk):
    k = pl.program_id(2)

    @pl.when(k == 0)
    def _():
        acc_ref[...] = jnp.zeros_like(acc_ref)

    acc_ref[...] += _nn(x_ref[...].astype(BF16), w_ref[...].astype(BF16))

    @pl.when(k == nk - 1)
    def _():
        o_ref[...] = acc_ref[...].astype(o_ref.dtype)


def _mm(x, w, *, tm, tn, tk, out_dtype, name):
    m, kd = x.shape
    n = w.shape[1]
    assert m % tm == 0 and n % tn == 0 and kd % tk == 0
    nk = kd // tk
    return pl.pallas_call(
        functools.partial(_mm_body, nk=nk),
        out_shape=jax.ShapeDtypeStruct((m, n), out_dtype),
        grid=(m // tm, n // tn, nk),
        in_specs=[pl.BlockSpec((tm, tk), lambda i, j, k: (i, k)),
                  pl.BlockSpec((tk, tn), lambda i, j, k: (k, j))],
        out_specs=pl.BlockSpec((tm, tn), lambda i, j, k: (i, j)),
        scratch_shapes=[pltpu.VMEM((tm, tn), F32)],
        compiler_params=_cparams(("parallel", "parallel", "arbitrary")), name=name)(x, w)


def _stack_rows(q_ref, n_groups, dq, rp):
    pieces = [q_ref[:, g * dq:(g + 1) * dq] for g in range(n_groups)]
    rows = sum(p.shape[0] for p in pieces)
    if rows < rp:
        pieces.append(jnp.zeros((rp - rows, dq), pieces[0].dtype))
    return jnp.concatenate(pieces, axis=0) if len(pieces) > 1 else pieces[0]


def _unstack_rows(o, n_groups, tq):
    pieces = [o[g * tq:(g + 1) * tq, :] for g in range(n_groups)]
    return jnp.concatenate(pieces, axis=1) if n_groups > 1 else pieces[0]


def _flash_body(*refs, cfg):
    (n_groups, dq, tq, tk, dv, rp, nkt, has_slopes, has_sel, has_gate, sel_bs, mode, window) = cfg
    refs = list(refs)
    q_ref = refs.pop(0)
    k_ref = refs.pop(0)
    v_ref = refs.pop(0)
    qpos_ref = refs.pop(0)
    slope_ref = refs.pop(0) if has_slopes else None
    sel_ref = refs.pop(0) if has_sel else None
    gate_ref = refs.pop(0) if has_gate else None
    o_ref, m_sc, l_sc, acc_sc = refs
    qi = pl.program_id(2)
    m_sc[...] = jnp.full_like(m_sc, NEG)
    l_sc[...] = jnp.zeros_like(l_sc)
    acc_sc[...] = jnp.zeros_like(acc_sc)
    qs = _stack_rows(q_ref, n_groups, dq, rp).astype(BF16)

    def tile(ki, masked):
        k0 = pl.multiple_of(ki * tk, tk)
        kt = k_ref[pl.ds(k0, tk), :].astype(BF16)
        s = _nt(kt, qs)
        kpos = k0 + lax.broadcasted_iota(I32, (tk, 1), 0)
        if has_slopes:
            s = s + slope_ref[...] * kpos.astype(F32)
        if has_sel:
            nb = tk // sel_bs
            rows = [sel_ref[pl.ds(ki * nb + j, 1), :] for j in range(nb)]
            if nb == 1:
                s = s + rows[0]
            else:
                s = s + jnp.concatenate([jnp.broadcast_to(r, (sel_bs, rp)) for r in rows], axis=0)
        if masked:
            d = qpos_ref[...] - kpos
            mask = d >= 0
            if window is not None:
                mask = mask & (d <= window)
            s = jnp.where(mask, s, NEG)
        m_prev = m_sc[...]
        m_new = jnp.maximum(m_prev, jnp.max(s, axis=0, keepdims=True))
        alpha = jnp.exp(m_prev - m_new)
        p = jnp.exp(s - m_new)
        if masked:
            p = jnp.where(s > 0.5 * NEG, p, 0.0)
        l_sc[...] = alpha * l_sc[...] + jnp.sum(p, axis=0, keepdims=True)
        vt = v_ref[:, pl.ds(k0, tk)].astype(BF16)
        acc_sc[...] = alpha * acc_sc[...] + _nn(vt, p.astype(BF16))
        m_sc[...] = m_new

    if mode == 'causal':
        n_int = (qi * tq) // tk
        lax.fori_loop(0, n_int, lambda ki, c: (tile(ki, False), c)[1], 0)
        for r in range(tq // tk):
            tile(n_int + r, True)
    elif mode == 'window':
        klo = jnp.maximum(qi * tq - window, 0) // tk
        khi = ((qi + 1) * tq - 1) // tk
        lax.fori_loop(klo, khi + 1, lambda ki, c: (tile(ki, True), c)[1], 0)
    else:
        for ki in range(nkt):
            tile(ki, False)

    inv = jnp.where(m_sc[...] > 0.5 * NEG, 1.0 / jnp.maximum(l_sc[...], TINY), 0.0)
    if has_gate:
        inv = inv * gate_ref[...]
    o = (acc_sc[...] * inv).T
    o_ref[...] = _unstack_rows(o, n_groups, tq).astype(o_ref.dtype)


def _flash(q, k, vt, qpos, *, n_groups, dq, dk, dv, tq, tk, kblk, vblk, mode='causal', slopes=None,
           sel=None, sel_bs=None, gates=None, window=None, out_dtype=BF16, name='flash'):
    nb, t_q, wq = q.shape
    t_k = k.shape[1]
    n_g = wq // (n_groups * dq)
    nqt, nkt = t_q // tq, t_k // tk
    rp = qpos.shape[-1]
    assert t_q % tq == 0 and t_k % tk == 0 and tq % tk == 0 and rp >= n_groups * tq and rp % LANE == 0
    in_specs = [pl.BlockSpec((None, tq, n_groups * dq), lambda b, g, qi: (b, qi, g)),
                pl.BlockSpec((None, t_k, dk), lambda b, g, qi: (b, 0, kblk(g))),
                pl.BlockSpec((None, dv, t_k), lambda b, g, qi: (b, vblk(g), 0)),
                pl.BlockSpec((None, 1, rp), lambda b, g, qi: (qi, 0, 0))]
    args = [q, k, vt, qpos]
    if slopes is not None:
        in_specs.append(pl.BlockSpec((None, 1, rp), lambda b, g, qi: (g, 0, 0)))
        args.append(slopes)
    if sel is not None:
        nblk = sel.shape[3]
        in_specs.append(pl.BlockSpec((None, None, None, nblk, rp), lambda b, g, qi: (b, g, qi, 0, 0)))
        args.append(sel)
    if gates is not None:
        in_specs.append(pl.BlockSpec((None, None, None, 1, rp), lambda b, g, qi: (b, g, qi, 0, 0)))
        args.append(gates)
    cfg = (n_groups, dq, tq, tk, dv, rp, nkt, slopes is not None, sel is not None, gates is not None, sel_bs,
           mode, window)
    return pl.pallas_call(
        functools.partial(_flash_body, cfg=cfg),
        out_shape=jax.ShapeDtypeStruct((nb, t_q, n_g * n_groups * dv), out_dtype),
        grid=(nb, n_g, nqt), in_specs=in_specs,
        out_specs=pl.BlockSpec((None, tq, n_groups * dv), lambda b, g, qi: (b, qi, g)),
        scratch_shapes=[pltpu.VMEM((1, rp), F32), pltpu.VMEM((1, rp), F32), pltpu.VMEM((dv, rp), F32)],
        compiler_params=_cparams(("parallel", "parallel", "parallel")), name=name)(*args)


def _qpos_table(q_start, t_q, tq, n_groups, rp):
    nqt = t_q // tq
    base = q_start + np.arange(nqt)[:, None] * tq + np.tile(np.arange(tq), n_groups)[None, :]
    out = np.full((nqt, 1, rp), -1, np.int32)
    out[:, 0, :n_groups * tq] = base
    return jnp.asarray(out)


def _lane_table(per_group_vals, tq, rp):
    vals = np.asarray(per_group_vals, np.float32)
    n_g, n_groups = vals.shape
    out = np.zeros((n_g, 1, rp), np.float32)
    out[:, 0, :n_groups * tq] = np.repeat(vals, tq, axis=1)
    return jnp.asarray(out)


def _kmean_body(k_ref, o_ref):
    k = k_ref[...].astype(F32)
    w = k.shape[-1]
    o_ref[...] = jnp.sum(k.reshape(SUBLANE, MOBA_BLOCK, w), axis=1) * (1.0 / MOBA_BLOCK)


def _moba_kmean(k, n_mean):
    nb, _, w = k.shape
    assert n_mean % SUBLANE == 0
    return pl.pallas_call(
        _kmean_body, out_shape=jax.ShapeDtypeStruct((nb, n_mean, w), F32), grid=(nb, n_mean // SUBLANE),
        in_specs=[pl.BlockSpec((None, SUBLANE * MOBA_BLOCK, w), lambda b, c: (b, c, 0))],
        out_specs=pl.BlockSpec((None, SUBLANE, w), lambda b, c: (b, c, 0)),
        compiler_params=_cparams(("parallel", "parallel")), name='moba_kmean')(k)


def _rank_desc(vals, n_cand, row_iota):
    rank = jnp.zeros(vals.shape, I32)
    for i in range(n_cand):
        vi = vals[i:i + 1, :]
        rank = rank + ((vi > vals) | ((vi == vals) & (i < row_iota))).astype(I32)
    return rank


def _moba_gate_body(q_ref, km_ref, qpos_ref, o_ref, *, n_groups, dq, rp, n_cand):
    qs = _stack_rows(q_ref, n_groups, dq, rp)
    gate = _nt(km_ref[...], qs, precision=HIGHEST)
    nblk_pad = o_ref.shape[0]
    if nblk_pad > gate.shape[0]:
        gate = jnp.concatenate([gate, jnp.zeros((nblk_pad - gate.shape[0], rp), F32)], axis=0)
    own = qpos_ref[...] // MOBA_BLOCK
    n_iota = lax.broadcasted_iota(I32, gate.shape, 0)
    elig = n_iota < own
    g = jnp.where(elig, gate, NEG)
    rank = _rank_desc(g, n_cand, n_iota)
    selected = (elig & (rank < MOBA_TOPK)) | (n_iota == own)
    o_ref[...] = jnp.where(selected, 0.0, NEG)


def _moba_gate(qf, kmean, qpos, *, n_groups, dq, dk, tq, kblk, n_cand, nblk_pad):
    nb, t_q, wq = qf.shape
    n_g = wq // (n_groups * dq)
    nqt = t_q // tq
    rp = qpos.shape[-1]
    n_mean = kmean.shape[1]
    return pl.pallas_call(
        functools.partial(_moba_gate_body, n_groups=n_groups, dq=dq, rp=rp, n_cand=n_cand),
        out_shape=jax.ShapeDtypeStruct((nb, n_g, nqt, nblk_pad, rp), F32), grid=(nb, n_g, nqt),
        in_specs=[pl.BlockSpec((None, tq, n_groups * dq), lambda b, g, qi: (b, qi, g)),
                  pl.BlockSpec((None, n_mean, dk), lambda b, g, qi: (b, 0, kblk(g))),
                  pl.BlockSpec((None, 1, rp), lambda b, g, qi: (qi, 0, 0))],
        out_specs=pl.BlockSpec((None, None, None, nblk_pad, rp), lambda b, g, qi: (b, g, qi, 0, 0)),
        compiler_params=_cparams(("parallel", "parallel", "parallel")), name='moba_gate')(qf, kmean, qpos)


def _pad_head_cols(w, kv_of_head, slot=LANE):
    d = w.shape[0]
    n_h = len(kv_of_head)
    per = slot // HEAD_DIM
    out = jnp.zeros((d, n_h, per, HEAD_DIM), w.dtype)
    w3 = w.reshape(d, n_h, HEAD_DIM)
    for h, kv in enumerate(kv_of_head):
        out = out.at[:, h, kv % per, :].set(w3[:, h, :])
    return out.reshape(d, n_h * slot)


def _pad_head_rows(w, kv_of_head, slot=LANE):
    return _pad_head_cols(w.T, kv_of_head, slot).T


_KV_OF_HEAD = [h // GQA for h in range(N_HEADS)]
_KV_OF_MEM = list(range(N_MEM_HEADS))


def _moba_prompt_attend(q, qf, kv, vt, slopes, *, tq=256):
    b, t, _ = q.shape
    tk = MOBA_BLOCK
    nblk = t // MOBA_BLOCK
    rp = GQA * tq
    qpos = _qpos_table(0, t, tq, GQA, rp)
    n_mean = _round_up(nblk, SUBLANE)
    kmean = _moba_kmean(kv, n_mean)
    pair = lambda g: g // 2
    sel = _moba_gate(qf, kmean, qpos, n_groups=GQA, dq=LANE, dk=LANE, tq=tq, kblk=pair, n_cand=nblk - 1,
                     nblk_pad=n_mean)
    slope_tab = _lane_table(slopes.reshape(N_KV_HEADS, GQA), tq, rp)
    return _flash(q, kv, vt, qpos, n_groups=GQA, dq=LANE, dk=LANE, dv=LANE, tq=tq, tk=tk, kblk=pair, vblk=pair,
                  slopes=slope_tab, sel=sel, sel_bs=MOBA_BLOCK, out_dtype=BF16, name='moba_flash')


def _compress_weight(phi_w1):
    w1r = phi_w1.reshape(2, 2, CMP_STRIDE, HEAD_DIM, CMP_HIDDEN)
    eye_c = jnp.eye(2, dtype=phi_w1.dtype)
    eye_k = jnp.eye(N_KV_HEADS, dtype=phi_w1.dtype)
    w = jnp.einsum('chpdx,ce,kl->pckdehlx', w1r, eye_c, eye_k)
    return w.reshape(CMP_STRIDE * 2 * KVW, 2 * 2 * N_KV_HEADS * CMP_HIDDEN)


def _block_diag4(w):
    return jnp.einsum('xy,kl->kxly', w, jnp.eye(N_KV_HEADS, dtype=w.dtype)).reshape(KVW, KVW)


def _compress2_body(fs_ref, pe_ref, w1_ref, b1_ref, w2k_ref, w2vt_ref, b2k_ref, b2v_ref, kc_ref, vct_ref):
    fs = fs_ref[...]
    n_sub = fs.shape[0]
    outs = []
    for c in range(2):
        first = fs[:, (2 * c) * KVW:(2 * c + 1) * KVW]
        second = fs[:, (2 * c + 1) * KVW:(2 * c + 2) * KVW]
        second = pltpu.roll(second, shift=n_sub - 1, axis=0)
        c1 = _nn(pe_ref[c], w1_ref[c], precision=HIGHEST)[0:1, :] + b1_ref[c]
        hid = jax.nn.gelu(first + second + _tile_lanes(c1, N_KV_HEADS))
        outs.append(hid.astype(BF16))
    kc_ref[...] = (_nn(outs[0], w2k_ref[...]) + b2k_ref[...]).astype(kc_ref.dtype)
    vct_ref[...] = (_nt(w2vt_ref[...], outs[1]) + b2v_ref[...]).astype(vct_ref.dtype)


def _nsa_compress(cmp_rows, cmp_pe, phi_w1, phi_b1, phi_w2, phi_b2):
    nb, length, _ = cmp_rows.shape
    n_sub = length // CMP_STRIDE
    stacked = cmp_rows.reshape(nb * n_sub, CMP_STRIDE * 2 * KVW)
    m = nb * n_sub
    tm = 512 if m % 512 == 0 else n_sub
    fs = _mm(stacked, _compress_weight(phi_w1).astype(BF16), tm=tm, tn=512, tk=2048, out_dtype=F32,
             name='nsa_compress_mm').reshape(nb, n_sub, 4 * KVW)
    return _compress_stage2(fs, cmp_pe, phi_w1, phi_b1, phi_w2, phi_b2)


def _compress_stage2(fs, cmp_pe, phi_w1, phi_b1, phi_w2, phi_b2):
    nb, n_sub, _ = fs.shape
    pe = jnp.zeros((2, SUBLANE, CMP_BLOCK * HEAD_DIM), F32).at[:, 0, :].set(cmp_pe.reshape(2, -1))
    w2k = _block_diag4(phi_w2[0]).astype(BF16)
    w2vt = _block_diag4(phi_w2[1]).T.astype(BF16)
    b2k = jnp.tile(phi_b2[0], N_KV_HEADS).reshape(1, KVW)
    b2v = jnp.tile(phi_b2[1], N_KV_HEADS).reshape(KVW, 1)
    const = lambda a: pl.BlockSpec(a.shape, lambda b, nd=a.ndim: (0,) * nd)
    b1 = phi_b1.reshape(2, 1, CMP_HIDDEN)
    consts = [pe, phi_w1, b1, w2k, w2vt, b2k, b2v]
    return pl.pallas_call(
        _compress2_body,
        out_shape=[jax.ShapeDtypeStruct((nb, n_sub, KVW), BF16), jax.ShapeDtypeStruct((nb, KVW, n_sub), BF16)],
        grid=(nb,),
        in_specs=[pl.BlockSpec((None, n_sub, 4 * KVW), lambda b: (b, 0, 0))] + [const(a) for a in consts],
        out_specs=[pl.BlockSpec((None, n_sub, KVW), lambda b: (b, 0, 0)),
                   pl.BlockSpec((None, KVW, n_sub), lambda b: (b, 0, 0))],
        compiler_params=_cparams(("parallel",)), name='nsa_compress_mlp')(fs, *consts)


def _nsa_cmp_body(q_ref, kc_ref, vct_ref, qpos_ref, slope_ref, gate_ref, cover_ref, gsum_ref, o_ref, sel_ref, *,
                  n_groups, dq, tq, rp, n_cmp, n_slc, n_top, lane_sum):
    qs = _stack_rows(q_ref, n_groups, dq, rp).astype(BF16)
    s = _nt(kc_ref[...], qs)
    n_sub = s.shape[0]
    n_iota = lax.broadcasted_iota(I32, (n_sub, 1), 0)
    qpos = qpos_ref[...]
    d = qpos - (n_iota * CMP_STRIDE + (CMP_BLOCK - 1))
    s = s - slope_ref[...] * d.astype(F32)
    valid = (d >= 0) & (n_iota < n_cmp)
    s = jnp.where(valid, s, NEG)
    m = jnp.max(s, axis=0, keepdims=True)
    e = jnp.where(valid, jnp.exp(s - m), 0.0)
    p = e / jnp.maximum(jnp.sum(e, axis=0, keepdims=True), TINY)
    o = (_nn(vct_ref[...], p.astype(BF16)) * gate_ref[...]).T
    o_ref[...] = _unstack_rows(o, n_groups, tq).astype(o_ref.dtype)
    imp = _nn(cover_ref[...], p, precision=HIGHEST)
    if lane_sum:
        tot = imp[:, 0:tq]
        for g in range(1, n_groups):
            tot = tot + imp[:, g * tq:(g + 1) * tq]
        imp = _tile_lanes(tot, n_groups)
    else:
        imp = _nn(imp, gsum_ref[...], precision=HIGHEST)
    cur = qpos // SEL_BLOCK
    jj = lax.broadcasted_iota(I32, imp.shape, 0)
    forced = (jj == 0) | (jj == cur) | (jj == cur - 1)
    imp = jnp.where(jj > cur, NEG, jnp.where(forced, FORCE, imp))
    rank = _rank_desc(imp, n_slc, jj)
    sel_ref[...] = jnp.where((rank < n_top) & (jj < n_slc), 0.0, NEG)


def _nsa_cmp(q, kc, vct, qpos, slopes, gates, *, n_groups, dq, dk, dv, tq, kblk, n_cmp, n_slc, group_of_lane):
    nb, t_q, wq = q.shape
    n_g = wq // (n_groups * dq)
    nqt = t_q // tq
    rp = qpos.shape[-1]
    n_sub = kc.shape[1]
    n_slc_pad = _round_up(n_slc, SUBLANE)
    ci = np.arange(n_sub)[None, :] * CMP_STRIDE
    sj = np.arange(n_slc_pad)[:, None] * SEL_BLOCK
    cover_t = ((ci < sj + SEL_BLOCK) & (ci + CMP_BLOCK > sj) & (np.arange(n_sub)[None, :] < n_cmp)
               & (np.arange(n_slc_pad)[:, None] < n_slc)).astype(np.float32)
    lane_sum = tq % LANE == 0
    gl = np.asarray(group_of_lane)
    gsum = ((gl[:, None] == gl[None, :]) & (gl[:, None] >= 0)).astype(np.float32)
    n_top = min(N_SEL, n_slc)
    body = functools.partial(_nsa_cmp_body, n_groups=n_groups, dq=dq, tq=tq, rp=rp, n_cmp=n_cmp, n_slc=n_slc,
                             n_top=n_top, lane_sum=lane_sum)
    gidx = lambda b, g, qi: (b, g, qi, 0, 0)
    return pl.pallas_call(
        body,
        out_shape=[jax.ShapeDtypeStruct((nb, t_q, n_g * n_groups * dv), F32),
                   jax.ShapeDtypeStruct((nb, n_g, nqt, n_slc_pad, rp), F32)],
        grid=(nb, n_g, nqt),
        in_specs=[pl.BlockSpec((None, tq, n_groups * dq), lambda b, g, qi: (b, qi, g)),
                  pl.BlockSpec((None, n_sub, dk), lambda b, g, qi: (b, 0, kblk(g))),
                  pl.BlockSpec((None, dv, n_sub), lambda b, g, qi: (b, kblk(g), 0)),
                  pl.BlockSpec((None, 1, rp), lambda b, g, qi: (qi, 0, 0)),
                  pl.BlockSpec((None, 1, rp), lambda b, g, qi: (g, 0, 0)),
                  pl.BlockSpec((None, None, None, 1, rp), gidx),
                  pl.BlockSpec((n_slc_pad, n_sub), lambda b, g, qi: (0, 0)),
                  pl.BlockSpec((rp, rp), lambda b, g, qi: (0, 0))],
        out_specs=[pl.BlockSpec((None, tq, n_groups * dv), lambda b, g, qi: (b, qi, g)),
                   pl.BlockSpec((None, None, None, n_slc_pad, rp), gidx)],
        compiler_params=_cparams(("parallel", "parallel", "parallel")), name='nsa_cmp')(
            q, kc, vct, qpos, slopes, gates, jnp.asarray(cover_t), jnp.asarray(gsum))


def _gate_lanes(g_t, b, t, tq):
    nqt = t // tq
    g = g_t[:3 * N_HEADS].reshape(3, N_KV_HEADS, GQA, b, nqt, tq)
    return jnp.transpose(g, (0, 3, 1, 4, 2, 5)).reshape(3, b, N_KV_HEADS, nqt, 1, GQA * tq)


def _nsa_prompt_attend(q, cmp_kv, sel_kv, win_kv, vt_sel, vt_win, gates, slopes, phi, *, tq=256):
    b, t, _ = q.shape
    rp = GQA * tq
    qpos = _qpos_table(0, t, tq, GQA, rp)
    slope_tab = _lane_table(slopes.reshape(N_KV_HEADS, GQA), tq, rp)
    pair = lambda g: g // 2
    kc, vct = _nsa_compress(cmp_kv, *phi)
    n_cmp = t // CMP_STRIDE - 1
    n_slc = t // SEL_BLOCK
    group_of_lane = np.tile(np.arange(tq), GQA)
    o_cmp, sel = _nsa_cmp(q, kc, vct, qpos, slope_tab, gates[0], n_groups=GQA, dq=LANE, dk=LANE, dv=LANE, tq=tq,
                          kblk=pair, n_cmp=n_cmp, n_slc=n_slc, group_of_lane=group_of_lane)
    tk = 256
    o_sel = _flash(q, sel_kv, vt_sel, qpos, n_groups=GQA, dq=LANE, dk=LANE, dv=LANE, tq=tq, tk=tk, kblk=pair,
                   vblk=pair, slopes=slope_tab, sel=sel, sel_bs=SEL_BLOCK, gates=gates[1],
                   out_dtype=F32, name='nsa_sel_flash')
    o_win = _flash(q, win_kv, vt_win, qpos, n_groups=GQA, dq=LANE, dk=LANE, dv=LANE, tq=tq, tk=tk, kblk=pair,
                   vblk=pair, mode='window', slopes=slope_tab, gates=gates[2], window=WINDOW,
                   out_dtype=F32, name='nsa_win_flash')
    return o_cmp, o_sel, o_win


def _inproj_body(*refs, plan, n_const):
    x_ref = refs[0]
    consts = refs[1:1 + n_const]
    outs = refs[1 + n_const:]
    xb = x_ref[...].astype(BF16)
    y = None
    for (kind, wi, bi, scale), o_ref in zip(plan, outs):
        if kind == 'mm':
            y = _nn(xb, consts[wi][...])
            if scale != 1.0:
                y = y * scale
        elif kind == 'mmT':
            y = _nt(consts[wi][...], xb)
        elif kind == 'gateT':
            y = jax.nn.sigmoid(_nt(consts[wi][...], xb) + consts[bi][...])
        o_ref[...] = y.astype(o_ref.dtype)


def _inproj(x, consts, plan, outs, tm, name):
    body = functools.partial(_inproj_body, plan=plan, n_const=len(consts))
    return _row_call(body, x.shape[0], tm, [(x, 'row')] + [(c, 'const') for c in consts], outs, name)


def _rope_tables(pos):
    half = QK_ROPE // 2
    inv = ROPE_THETA ** (-np.arange(half, dtype=np.float32) / half)
    ang = pos.astype(np.float32)[:, None] * inv[None, :]
    cos = np.concatenate([np.cos(ang), np.cos(ang)], -1).astype(np.float32)
    sin = np.concatenate([np.sin(ang), np.sin(ang)], -1).astype(np.float32)
    n = pos.shape[0]
    ctab = np.concatenate([np.ones((n, QK_NOPE), np.float32), cos, np.zeros((n, LANE - QK_NOPE - QK_ROPE), np.float32)], -1)
    stab = np.concatenate([np.zeros((n, QK_NOPE), np.float32), sin, np.zeros((n, LANE - QK_NOPE - QK_ROPE), np.float32)], -1)
    return jnp.asarray(cos), jnp.asarray(sin), jnp.asarray(ctab), jnp.asarray(stab)


def _rot_cols(w):
    half = QK_ROPE // 2
    return jnp.concatenate([-w[:, half:], w[:, :half]], axis=1)


def _inproj_mla_body(x_ref, cos_ref, sin_ref, ctab_ref, stab_ref, wcq, wckv, wpe, wper, wmem, qn, kvn, wuq, wuqr,
                     *rest, prompt, scale):
    if prompt:
        wuk, epe, wuvt, ckv_o, kpe_o, q_o, qmem_o, kh_o, vht_o = rest
    else:
        ckv_o, kpe_o, q_o, qmem_o = rest
    xb = x_ref[...].astype(BF16)
    cq = _rmsnorm(_nn(xb, wcq[...]), qn[...])
    ckv = _rmsnorm(_nn(xb, wckv[...]), kvn[...])
    kpe = _nn(xb, wpe[...]) * cos_ref[...] + _nn(xb, wper[...]) * sin_ref[...]
    ckv_o[...] = ckv
    kpe_o[...] = kpe
    cqb = cq.astype(BF16)
    q = _nn(cqb, wuq[...]) * _tile_lanes(ctab_ref[...], N_HEADS) + _nn(cqb, wuqr[...]) * _tile_lanes(stab_ref[...], N_HEADS)
    q_o[...] = (q * scale).astype(q_o.dtype)
    qmem_o[...] = (_nn(xb, wmem[...]) * (HEAD_DIM ** -0.5)).astype(qmem_o.dtype)
    if prompt:
        ckvb = ckv.astype(BF16)
        kh_o[...] = (_nn(ckvb, wuk[...]) + _nn(kpe.astype(BF16), epe[...])).astype(kh_o.dtype)
        vht_o[...] = _nt(wuvt[...], ckvb).astype(vht_o.dtype)


def _mla_weights(w_in, q_norm, kv_norm, w_uq, w_uk, w_uv, wmem_pad, prompt):
    e = QK_NOPE + QK_ROPE
    wcq = w_in[:, :Q_LORA].astype(BF16)
    wckv = w_in[:, Q_LORA:Q_LORA + KV_LORA].astype(BF16)
    wpe_f = w_in[:, Q_LORA + KV_LORA:Q_LORA + KV_LORA + QK_ROPE]
    pad = jnp.zeros((Q_LORA, N_HEADS, LANE - e), F32)
    wuq = jnp.concatenate([w_uq, pad], -1).reshape(Q_LORA, N_HEADS * LANE).astype(BF16)
    rot = jnp.concatenate([jnp.zeros((Q_LORA, N_HEADS, QK_NOPE), F32),
                           jax.vmap(_rot_cols, in_axes=1, out_axes=1)(w_uq[:, :, QK_NOPE:]), pad], -1)
    wuqr = rot.reshape(Q_LORA, N_HEADS * LANE).astype(BF16)
    consts = [wcq, wckv, wpe_f.astype(BF16), _rot_cols(wpe_f).astype(BF16), wmem_pad,
              q_norm.reshape(1, -1), kv_norm.reshape(1, -1), wuq, wuqr]
    if prompt:
        wuk = jnp.concatenate([w_uk, jnp.zeros((KV_LORA, N_HEADS, LANE - QK_NOPE), F32)], -1)
        epe = jnp.zeros((QK_ROPE, N_HEADS, LANE), F32).at[:, :, QK_NOPE:e].set(
            jnp.broadcast_to(jnp.eye(QK_ROPE, dtype=F32)[:, None, :], (QK_ROPE, N_HEADS, QK_ROPE)))
        wuvt = w_uv.reshape(KV_LORA, HQ).T
        consts += [wuk.reshape(KV_LORA, N_HEADS * LANE).astype(BF16), epe.reshape(QK_ROPE, N_HEADS * LANE).astype(BF16),
                   wuvt.astype(BF16)]
    return consts


def _inproj_mla(x, pos_rows, consts, tm, prompt, mem_w):
    cos, sin, ctab, stab = _rope_tables(pos_rows)
    n = x.shape[0]
    outs = [(KV_LORA, F32, False), (QK_ROPE, F32, False), (N_HEADS * LANE, BF16, False), (mem_w, BF16, False)]
    if prompt:
        outs += [(N_HEADS * LANE, BF16, False), (HQ, BF16, True)]
    body = functools.partial(_inproj_mla_body, prompt=prompt, scale=(QK_NOPE + QK_ROPE) ** -0.5)
    ins = [(x, 'row'), (cos, 'period'), (sin, 'period'), (ctab, 'period'), (stab, 'period')] + [(c, 'const') for c in consts]
    return _row_call(body, n, tm, ins, outs, 'inproj_mla')


def _outproj_body(*refs, set_sizes):
    n_o = sum(set_sizes)
    n_sets = len(set_sizes)
    x_ref = refs[0]
    o_refs = refs[1:1 + n_o]
    w_refs = refs[1 + n_o:1 + n_o + n_sets]
    g_ref, b_ref, wr_ref, rb_ref, x1_o, x1b_o, e_o, gw_o = refs[1 + n_o + n_sets:]
    acc = None
    idx = 0
    for si, n in enumerate(set_sizes):
        a = o_refs[idx][...]
        for j in range(1, n):
            a = a + o_refs[idx + j][...]
        idx += n
        term = _nn(a.astype(BF16), w_refs[si][...])
        acc = term if acc is None else acc + term
    x1 = _layernorm(ALPHA * x_ref[...] + acc, g_ref[...], b_ref[...])
    x1_o[...] = x1
    x1b_o[...] = x1.astype(BF16)
    scores = jax.nn.sigmoid(_nt(wr_ref[...], x1, precision=HIGHEST))
    biased = scores + rb_ref[...]
    rb = [biased[i:i + 1, :] for i in range(N_EXPERTS)]
    rs = [scores[i:i + 1, :] for i in range(N_EXPERTS)]
    gbest, gidx = None, None
    for g in range(N_GROUPS):
        v = rb[g * EXPERTS_PER_GROUP:(g + 1) * EXPERTS_PER_GROUP]
        top2 = None
        for i in range(EXPERTS_PER_GROUP):
            for j in range(i + 1, EXPERTS_PER_GROUP):
                pair = v[i] + v[j]
                top2 = pair if top2 is None else jnp.maximum(top2, pair)
        if g == 0:
            gbest, gidx = top2, jnp.zeros(top2.shape, I32)
        else:
            better = top2 > gbest
            gidx = jnp.where(better, g, gidx)
            gbest = jnp.where(better, top2, gbest)

    def member(rows, i):
        out = rows[(N_GROUPS - 1) * EXPERTS_PER_GROUP + i]
        for g in range(N_GROUPS - 2, -1, -1):
            out = jnp.where(gidx == g, rows[g * EXPERTS_PER_GROUP + i], out)
        return out

    vb = [member(rb, i) for i in range(EXPERTS_PER_GROUP)]
    vs = [member(rs, i) for i in range(EXPERTS_PER_GROUP)]
    b1, i1 = vb[0], jnp.zeros(gidx.shape, I32)
    for i in range(1, EXPERTS_PER_GROUP):
        better = vb[i] > b1
        i1 = jnp.where(better, i, i1)
        b1 = jnp.where(better, vb[i], b1)
    b2, i2 = jnp.full(b1.shape, -3e38, F32), jnp.zeros(gidx.shape, I32)
    for i in range(EXPERTS_PER_GROUP):
        ok = (i1 != i) & (vb[i] > b2)
        i2 = jnp.where(ok, i, i2)
        b2 = jnp.where(ok, vb[i], b2)

    def pick(rows, idx_):
        out = rows[EXPERTS_PER_GROUP - 1]
        for i in range(EXPERTS_PER_GROUP - 2, -1, -1):
            out = jnp.where(idx_ == i, rows[i], out)
        return out

    w1, w2 = pick(vs, i1), pick(vs, i2)
    den = w1 + w2
    e_o[...] = jnp.concatenate([gidx * EXPERTS_PER_GROUP + i1, gidx * EXPERTS_PER_GROUP + i2], axis=0)
    gw_o[...] = jnp.concatenate([w1 / den, w2 / den], axis=0)


def _outproj(x, o_sets, ln_g, ln_b, w_router, router_bias, tm):
    n = x.shape[0]
    ins = [(x, 'row')]
    for arrs, _ in o_sets:
        ins += [(a, 'row') for a in arrs]
    ins += [(w, 'const') for _, w in o_sets]
    ins += [(ln_g.reshape(1, -1), 'const'), (ln_b.reshape(1, -1), 'const'), (w_router.T, 'const'),
            (router_bias.reshape(-1, 1), 'const')]
    outs = [(D_MODEL, F32, False), (D_MODEL, BF16, False), (TOP_K, I32, True), (TOP_K, F32, True)]
    body = functools.partial(_outproj_body, set_sizes=tuple(len(a) for a, _ in o_sets))
    return _row_call(body, n, tm, ins, outs, 'outproj_ln_router')


def _expert_body(be_ref, nu_ref, x_ref, wg_ref, wu_ref, wd_ref, o_ref, wg_sc, wu_sc, wd_sc):
    i = pl.program_id(0)
    changed = (i == 0) | (be_ref[jnp.maximum(i - 1, 0)] != be_ref[i])

    @pl.when(changed)
    def _():
        wg_sc[...] = wg_ref[...].astype(BF16)
        wu_sc[...] = wu_ref[...].astype(BF16)
        wd_sc[...] = wd_ref[...].astype(BF16)

    @pl.when(i < nu_ref[0])
    def _():
        xb = x_ref[...]
        h = jax.nn.silu(_nn(xb, wg_sc[...])) * _nn(xb, wu_sc[...])
        o_ref[...] = _nn(h.astype(BF16), wd_sc[...])

    @pl.when(i >= nu_ref[0])
    def _():
        o_ref[...] = jnp.zeros_like(o_ref)


def _experts(xg, blk_expert, n_used, w_g, w_u, w_d):
    n_slots = xg.shape[0]
    n_blocks = n_slots // MOE_TILE
    grid_spec = pltpu.PrefetchScalarGridSpec(
        num_scalar_prefetch=2, grid=(n_blocks,),
        in_specs=[pl.BlockSpec((MOE_TILE, D_MODEL), lambda i, be, nu: (i, 0)),
                  pl.BlockSpec((None, D_MODEL, D_EXPERT), lambda i, be, nu: (be[i], 0, 0)),
                  pl.BlockSpec((None, D_MODEL, D_EXPERT), lambda i, be, nu: (be[i], 0, 0)),
                  pl.BlockSpec((None, D_EXPERT, D_MODEL), lambda i, be, nu: (be[i], 0, 0))],
        out_specs=pl.BlockSpec((MOE_TILE, D_MODEL), lambda i, be, nu: (i, 0)),
        scratch_shapes=[pltpu.VMEM((D_MODEL, D_EXPERT), BF16), pltpu.VMEM((D_MODEL, D_EXPERT), BF16),
                        pltpu.VMEM((D_EXPERT, D_MODEL), BF16)])
    return pl.pallas_call(
        _expert_body, out_shape=jax.ShapeDtypeStruct((n_slots, D_MODEL), F32), grid_spec=grid_spec,
        compiler_params=_cparams(("arbitrary",)), name='moe_experts')(blk_expert, n_used, xg, w_g, w_u, w_d)


def _ln2_body(x_ref, y0_ref, y1_ref, gw_ref, g_ref, b_ref, o_ref):
    gw = gw_ref[...]
    y = gw[:, 0:1] * y0_ref[...] + gw[:, 1:2] * y1_ref[...]
    o_ref[...] = _layernorm(ALPHA * x_ref[...] + y, g_ref[...], b_ref[...])


def _moe(x1, x1b, expert_t, gate_t, w_g, w_u, w_d, ln_g, ln_b, tm):
    n = x1.shape[0]
    n_assign = TOP_K * n
    e_flat = expert_t.reshape(-1)
    onehot = (e_flat[:, None] == jnp.arange(N_EXPERTS)[None, :]).astype(I32)
    counts = onehot.sum(0)
    padded = (counts + MOE_TILE - 1) // MOE_TILE * MOE_TILE
    ends = jnp.cumsum(padded)
    starts = ends - padded
    rank = (jnp.cumsum(onehot, 0) * onehot).sum(-1) - 1
    dest = starts[e_flat] + rank
    n_blocks = -(-n_assign // MOE_TILE) + N_EXPERTS
    n_slots = n_blocks * MOE_TILE
    slot_tok = jnp.zeros((n_slots,), I32).at[dest].set(jnp.arange(n_assign, dtype=I32) % n)
    blk_expert = jnp.minimum((jnp.arange(n_blocks)[:, None] * MOE_TILE >= ends[None, :]).sum(-1), N_EXPERTS - 1).astype(I32)
    n_used = (ends[-1] // MOE_TILE).astype(I32).reshape(1)
    xg = jnp.take(x1b, slot_tok, axis=0)
    out = _experts(xg, blk_expert, n_used, w_g, w_u, w_d)
    y = jnp.take(out, dest, axis=0).reshape(TOP_K, n, D_MODEL)
    ins = [(x1, 'row'), (y[0], 'row'), (y[1], 'row'), (gate_t.T, 'row'),
           (ln_g.reshape(1, -1), 'const'), (ln_b.reshape(1, -1), 'const')]
    return _row_call(_ln2_body, n, tm, ins, [(D_MODEL, F32, False)], 'moe_combine_ln')[0]


def _row_tile(n, cap=256):
    tm = cap
    while n % tm:
        tm //= 2
    assert tm >= SUBLANE
    return tm


def _feat_major(a_t, b, t):
    return jnp.transpose(a_t.reshape(a_t.shape[0], b, t), (1, 0, 2))


def _prompt_trunk(x3, mem_prompt, p, slopes):
    b, t, _ = x3.shape
    n = b * t
    x = x3.reshape(n, D_MODEL)
    tm = _row_tile(n)
    pair = lambda g: g // 2
    ckv_rows, kpe_rows, moba_rows, cmp_rows, sel_rows, win_states, mem_kvs = [], [], [], [], [], [], []
    mem2d = mem_prompt.reshape(b * N_MEM, D_MODEL)
    for li in range(DEPTH):
        kind, j = li % 3, li // 3
        w_in = (p['w_in_a'], p['w_in_b'], p['w_in_c'])[kind][j]
        wmem_pad = _pad_head_cols(w_in[:, -MEMQ:], _KV_OF_MEM).astype(BF16)
        w_out = p['w_out'][li]
        wm = p['w_mem_kv'][li]
        mkv, vtm = _inproj(mem2d, [wm.astype(BF16), wm[:, MEMQ:].T.astype(BF16)],
                           [('mm', 0, None, 1.0), ('mmT', 1, None, 1.0)],
                           [(2 * MEMQ, F32, False), (MEMQ, BF16, True)], _row_tile(b * N_MEM), 'mem_kv')
        mem_kvs.append(mkv.reshape(b, N_MEM, 2, N_MEM_HEADS, HEAD_DIM))
        if kind == 0:
            consts = _mla_weights(w_in, p['q_norm_a'][j], p['kv_norm_a'][j], p['w_uq_a'][j], p['w_uk_a'][j],
                                  p['w_uv_a'][j], wmem_pad, True)
            ckv, kpe, q, qmem, kh, vht = _inproj_mla(x, np.arange(t), consts, tm, True, N_MEM_HEADS * LANE)
            ckv_rows.append(ckv.reshape(b, t, KV_LORA))
            kpe_rows.append(kpe.reshape(b, t, QK_ROPE))
            tq, tk = min(512, t), min(256, t)
            qpos = _qpos_table(0, t, tq, 1, tq)
            o_mix = _flash(q.reshape(b, t, -1), kh.reshape(b, t, -1), _feat_major(vht, b, t), qpos, n_groups=1,
                           dq=LANE, dk=LANE, dv=LANE, tq=tq, tk=tk, kblk=lambda g: g, vblk=pair,
                           out_dtype=BF16, name='mla_flash')
            o_sets = [([o_mix.reshape(n, -1)], _pad_head_rows(w_out[:HQ], list(range(N_HEADS))).astype(BF16))]
        elif kind == 1:
            consts = [_pad_head_cols(w_in[:, :HQ], _KV_OF_HEAD).astype(BF16), w_in[:, HQ:HQ + 2 * KVW].astype(BF16),
                      w_in[:, HQ + KVW:HQ + 2 * KVW].T.astype(BF16), wmem_pad]
            plan = [('mm', 0, None, HEAD_DIM ** -0.5), ('prev', 0, None, 1.0), ('mm', 1, None, 1.0),
                    ('mmT', 2, None, 1.0), ('mm', 3, None, HEAD_DIM ** -0.5)]
            outs = [(N_HEADS * LANE, BF16, False), (N_HEADS * LANE, F32, False), (2 * KVW, F32, False),
                    (KVW, BF16, True), (N_MEM_HEADS * LANE, BF16, False)]
            q, qf, kv, vt, qmem = _inproj(x, consts, plan, outs, tm, 'inproj_moba')
            moba_rows.append(kv.reshape(b, t, 2, N_KV_HEADS, HEAD_DIM))
            o_mix = _moba_prompt_attend(q.reshape(b, t, -1), qf.reshape(b, t, -1), kv.reshape(b, t, -1),
                                        _feat_major(vt, b, t), slopes, tq=min(256, t))
            o_sets = [([o_mix.reshape(n, -1)], _pad_head_rows(w_out[:HQ], _KV_OF_HEAD).astype(BF16))]
        else:
            c0 = HQ
            wg_t = jnp.zeros((40, D_MODEL), F32).at[:3 * N_HEADS].set(w_in[:, c0 + 6 * KVW:c0 + 6 * KVW + 3 * N_HEADS].T)
            bg = jnp.zeros((40, 1), F32).at[:3 * N_HEADS, 0].set(p['b_gate_c'][j])
            consts = [_pad_head_cols(w_in[:, :HQ], _KV_OF_HEAD).astype(BF16),
                      w_in[:, c0:c0 + 2 * KVW].astype(BF16), w_in[:, c0 + 2 * KVW:c0 + 4 * KVW].astype(BF16),
                      w_in[:, c0 + 4 * KVW:c0 + 6 * KVW].astype(BF16),
                      w_in[:, c0 + 3 * KVW:c0 + 4 * KVW].T.astype(BF16), w_in[:, c0 + 5 * KVW:c0 + 6 * KVW].T.astype(BF16),
                      wg_t.astype(BF16), bg, wmem_pad]
            plan = [('mm', 0, None, HEAD_DIM ** -0.5), ('mm', 1, None, 1.0), ('mm', 2, None, 1.0), ('mm', 3, None, 1.0),
                    ('mmT', 4, None, 1.0), ('mmT', 5, None, 1.0), ('gateT', 6, 7, 1.0), ('mm', 8, None, HEAD_DIM ** -0.5)]
            outs = [(N_HEADS * LANE, BF16, False), (2 * KVW, F32, False), (2 * KVW, F32, False), (2 * KVW, F32, False),
                    (KVW, BF16, True), (KVW, BF16, True), (40, F32, True), (N_MEM_HEADS * LANE, BF16, False)]
            q, cmp_kv, sel_kv, win_kv, vts, vtw, g_t, qmem = _inproj(x, consts, plan, outs, tm, 'inproj_nsa')
            kvs = (2, N_KV_HEADS, HEAD_DIM)
            cmp_rows.append(cmp_kv.reshape((b, t) + kvs))
            sel_rows.append(sel_kv.reshape((b, t) + kvs))
            keep = min(WINDOW, t)
            win_states.append(win_kv.reshape((b, t) + kvs)[:, t - keep:])
            tq = min(256, t)
            phi = (p['cmp_pe_c'][j], p['phi_w1_c'][j], p['phi_b1_c'][j], p['phi_w2_c'][j], p['phi_b2_c'][j])
            o3 = _nsa_prompt_attend(q.reshape(b, t, -1), cmp_kv.reshape(b, t, -1), sel_kv.reshape(b, t, -1),
                                    win_kv.reshape(b, t, -1), _feat_major(vts, b, t), _feat_major(vtw, b, t),
                                    _gate_lanes(g_t, b, t, tq), slopes, phi, tq=tq)
            o_sets = [([o.reshape(n, -1) for o in o3], _pad_head_rows(w_out[:HQ], _KV_OF_HEAD).astype(BF16))]
        tqm = min(512, t)
        o_mem = _flash(qmem.reshape(b, t, -1), mkv.reshape(b, N_MEM, -1), _feat_major(vtm, b, N_MEM),
                       _qpos_table(0, t, tqm, 1, tqm), n_groups=1, dq=LANE, dk=LANE, dv=LANE, tq=tqm, tk=N_MEM,
                       kblk=pair, vblk=pair, mode='full', out_dtype=BF16, name='mem_flash')
        o_sets.append(([o_mem.reshape(n, -1)], _pad_head_rows(w_out[HQ:], _KV_OF_MEM).astype(BF16)))
        x1, x1b, e_t, g_w = _outproj(x, o_sets, p['ln1_g'][li], p['ln1_b'][li], p['w_router'], p['router_bias'], tm)
        x = _moe(x1, x1b, e_t, g_w, p['w_e_gate'][li], p['w_e_up'][li], p['w_e_down'][li],
                 p['ln2_g'][li], p['ln2_b'][li], tm)
    return (x.reshape(b, t, D_MODEL), jnp.stack(ckv_rows, 1), jnp.stack(kpe_rows, 1), jnp.stack(moba_rows, 1),
            jnp.stack(cmp_rows, 1), jnp.stack(sel_rows, 1), jnp.stack(win_states, 0), jnp.stack(mem_kvs, 0))


_SLOT_S = 4 * HEAD_DIM
PAGES_PER_STEP = 16


def _decode_body(*refs, cfg):
    (mode, n_groups, dq, ts, n_pp, pw, n_steps, dv, has_new, sel_bs, has_gate, has_slopes, causal, window,
     kpos0, past) = cfg
    refs = list(refs)
    refs.pop(0)
    q_ref = refs.pop(0)
    k_refs = [refs.pop(0) for _ in range(n_pp)]
    k2_refs = [refs.pop(0) for _ in range(n_pp)] if mode == 'mla' else [None] * n_pp
    v_refs = [refs.pop(0) for _ in range(n_pp)] if mode == 'ft' else k_refs
    if has_new:
        knew_ref = refs.pop(0)
        k2new_ref = refs.pop(0) if mode == 'mla' else None
        vnew_ref = refs.pop(0) if mode == 'ft' else knew_ref
    sel_ref = refs.pop(0) if sel_bs else None
    gate_ref = refs.pop(0) if has_gate else None
    qpos_ref = refs.pop(0)
    slope_ref = refs.pop(0) if has_slopes else None
    o_ref, m_sc, l_sc, acc_sc = refs
    rows = n_groups * ts
    step = pl.program_id(1)

    @pl.when(step == 0)
    def _():
        m_sc[...] = jnp.full_like(m_sc, NEG)
        l_sc[...] = jnp.zeros_like(l_sc)
        acc_sc[...] = jnp.zeros_like(acc_sc)

    q = _stack_rows(q_ref, n_groups, dq, rows)
    qpos = qpos_ref[...]

    def scores(k_ref, k2_ref):
        if mode == 'ft':
            return _nn(q, k_ref[...].astype(BF16))
        return (_nt(q[:, :KV_LORA], k_ref[...].astype(BF16))
                + _nn(q[:, KV_LORA:KV_LORA + QK_ROPE], k2_ref[...].astype(BF16)))

    def pv(p, v_ref):
        if mode == 'ft':
            return _nt(p, v_ref[...].astype(BF16))
        return _nn(p, v_ref[...].astype(BF16))

    def update(tiles, vrefs, masked):
        m_prev = m_sc[...]
        mx = tiles[0]
        for t in tiles[1:]:
            mx = jnp.maximum(mx, t)
        m_new = jnp.maximum(m_prev, jnp.max(mx, axis=1, keepdims=True))
        alpha = jnp.exp(m_prev - m_new)
        psum, acc = None, None
        for s, v_ref in zip(tiles, vrefs):
            p = jnp.exp(s - m_new)
            if masked:
                p = jnp.where(s > 0.5 * NEG, p, 0.0)
            psum = p if psum is None else psum + p
            term = pv(p.astype(BF16), v_ref)
            acc = term if acc is None else acc + term
        l_sc[...] = alpha * l_sc[...] + jnp.sum(psum, axis=1, keepdims=True)
        acc_sc[...] = alpha * acc_sc[...] + acc
        m_sc[...] = m_new

    lane = lax.broadcasted_iota(I32, (1, pw), 1)
    tiles = []
    for p in range(n_pp):
        s = scores(k_refs[p], k2_refs[p])
        kpos = kpos0 + (step * n_pp + p) * pw + lane
        d = qpos - kpos
        if has_slopes:
            s = s - slope_ref[...] * d.astype(F32)
        if sel_bs:
            if sel_bs >= pw:
                c = p // (sel_bs // pw)
                s = s + sel_ref[:, c:c + 1]
            else:
                per = pw // sel_bs
                bias = sel_ref[:, p * per:p * per + 1]
                for j in range(1, per):
                    bias = jnp.where(lane >= j * sel_bs, sel_ref[:, p * per + j:p * per + j + 1], bias)
                s = s + bias
        if window is not None:
            s = jnp.where(d <= window, s, NEG)
        tiles.append(s)
    update(tiles, v_refs, masked=bool(sel_bs) or window is not None)

    @pl.when(step == n_steps - 1)
    def _():
        if has_new:
            lane_n = lax.broadcasted_iota(I32, (1, knew_ref.shape[-1] if mode == 'ft' else knew_ref.shape[0]), 1)
            s = scores(knew_ref, k2new_ref)
            d = qpos - (past + lane_n)
            if has_slopes:
                s = s - slope_ref[...] * d.astype(F32)
            s = jnp.where(d >= 0, s, NEG)
            update([s], [vnew_ref], masked=True)
        inv = jnp.where(m_sc[...] > 0.5 * NEG, 1.0 / jnp.maximum(l_sc[...], TINY), 0.0)
        if has_gate:
            inv = inv * gate_ref[...]
        o_ref[...] = _unstack_rows(acc_sc[...] * inv, n_groups, ts).astype(o_ref.dtype)


def _decode(q, page_table, k_pages, v_pages, *, mode, j, n_groups, dq, dv, n_pp, pw, n_steps, paged, k2_pages=None,
            new=None, sel=None, sel_bs=None, gates=None, qpos=None, slopes=None, window=None, kpos0=0, past=0,
            out_dtype=BF16, name='decode'):
    db, ts, _ = q.shape
    rows = n_groups * ts
    in_specs = [pl.BlockSpec((None, ts, n_groups * dq), lambda b, s, pt: (b, 0, 0))]
    args = [q]

    def add_pages(arr, c):
        for pg in range(n_pp):
            if paged:
                if c is None:
                    spec = pl.BlockSpec((None, None) + arr.shape[2:],
                                        lambda b, s, pt, pg=pg: (pt[b * (n_pp * n_steps) + s * n_pp + pg], j, 0, 0))
                else:
                    spec = pl.BlockSpec((None, None, None) + arr.shape[3:],
                                        lambda b, s, pt, pg=pg, c=c: (pt[b * (n_pp * n_steps) + s * n_pp + pg], j, c, 0, 0))
            else:
                spec = pl.BlockSpec((None, None, None, arr.shape[3], pw),
                                    lambda b, s, pt, pg=pg, c=c: (j, b, c, 0, s * n_pp + pg))
            in_specs.append(spec)
            args.append(arr)

    if mode == 'ft':
        add_pages(k_pages, 0)
        add_pages(v_pages, 1)
    else:
        add_pages(k_pages, None)
        add_pages(k2_pages, None)
    if new is not None:
        for a in new:
            in_specs.append(pl.BlockSpec((None,) + a.shape[1:], lambda b, s, pt, nd=a.ndim: (b,) + (0,) * (nd - 1)))
            args.append(a)
    if sel is not None:
        in_specs.append(pl.BlockSpec((None, None, rows, sel.shape[-1]), lambda b, s, pt: (b, s, 0, 0)))
        args.append(sel)
    if gates is not None:
        in_specs.append(pl.BlockSpec((None, rows, 1), lambda b, s, pt: (b, 0, 0)))
        args.append(gates)
    in_specs.append(pl.BlockSpec((rows, 1), lambda b, s, pt: (0, 0)))
    args.append(qpos)
    if slopes is not None:
        in_specs.append(pl.BlockSpec((rows, 1), lambda b, s, pt: (0, 0)))
        args.append(slopes)
    cfg = (mode, n_groups, dq, ts, n_pp, pw, n_steps, dv, new is not None, sel_bs if sel is not None else None,
           gates is not None, slopes is not None, True, window, kpos0, past)
    grid_spec = pltpu.PrefetchScalarGridSpec(
        num_scalar_prefetch=1, grid=(db, n_steps), in_specs=in_specs,
        out_specs=pl.BlockSpec((None, ts, n_groups * dv), lambda b, s, pt: (b, 0, 0)),
        scratch_shapes=[pltpu.VMEM((rows, 1), F32), pltpu.VMEM((rows, 1), F32), pltpu.VMEM((rows, dv), F32)])
    return pl.pallas_call(
        functools.partial(_decode_body, cfg=cfg),
        out_shape=jax.ShapeDtypeStruct((db, ts, n_groups * dv), out_dtype), grid_spec=grid_spec,
        compiler_params=_cparams(("parallel", "arbitrary")), name=name)(page_table.reshape(-1), *args)


def _kmean_pages_body(pt_ref, *refs, n_pp):
    k_refs = refs[:n_pp]
    o_ref = refs[n_pp]
    step = pl.program_id(1)

    @pl.when(step == 0)
    def _():
        o_ref[...] = jnp.zeros_like(o_ref)

    lane = lax.broadcasted_iota(I32, (1, o_ref.shape[-1]), 1)
    per = MOBA_BLOCK // PAGE_SIZE
    acc = o_ref[...]
    for i in range(n_pp // per):
        tot = k_refs[per * i][...]
        for r in range(1, per):
            tot = tot + k_refs[per * i + r][...]
        mean = jnp.sum(tot, axis=1, keepdims=True) * (1.0 / MOBA_BLOCK)
        acc = jnp.where(lane == step * (n_pp // per) + i, mean, acc)
    o_ref[...] = acc


def _kmean_pages(k_pages, page_table, j, n_pp, n_steps):
    db = page_table.shape[0]
    f = k_pages.shape[3]
    in_specs = [pl.BlockSpec((None, None, None, f, PAGE_SIZE),
                             lambda b, s, pt, pg=pg: (pt[b * (n_pp * n_steps) + s * n_pp + pg], j, 0, 0, 0))
                for pg in range(n_pp)]
    grid_spec = pltpu.PrefetchScalarGridSpec(
        num_scalar_prefetch=1, grid=(db, n_steps), in_specs=in_specs,
        out_specs=pl.BlockSpec((None, f, LANE), lambda b, s, pt: (b, 0, 0)))
    return pl.pallas_call(
        functools.partial(_kmean_pages_body, n_pp=n_pp), out_shape=jax.ShapeDtypeStruct((db, f, LANE), F32),
        grid_spec=grid_spec, compiler_params=_cparams(("parallel", "arbitrary")), name='moba_kmean_pages')(
            page_table.reshape(-1), *([k_pages] * n_pp))


def _gate_rows_body(q_ref, kmt_ref, qpos_ref, o_ref, *, n_groups, dq, ts, n_cand):
    q = _stack_rows(q_ref, n_groups, dq, n_groups * ts)
    gate = _nn(q, kmt_ref[...], precision=HIGHEST)
    own = qpos_ref[...] // MOBA_BLOCK
    lane = lax.broadcasted_iota(I32, gate.shape, 1)
    elig = lane < own
    g = jnp.where(elig, gate, NEG)
    rank = jnp.zeros(g.shape, I32)
    for i in range(n_cand):
        gi = g[:, i:i + 1]
        rank = rank + ((gi > g) | ((gi == g) & (i < lane))).astype(I32)
    selected = (elig & (rank < MOBA_TOPK)) | (lane == own)
    o_ref[...] = jnp.where(selected, 0.0, NEG)


def _gate_rows(qf, kmean_t, qpos, *, n_groups, dq, n_cand):
    db, ts, _ = qf.shape
    rows = n_groups * ts
    return pl.pallas_call(
        functools.partial(_gate_rows_body, n_groups=n_groups, dq=dq, ts=ts, n_cand=n_cand),
        out_shape=jax.ShapeDtypeStruct((db, rows, LANE), F32), grid=(db,),
        in_specs=[pl.BlockSpec((None, ts, n_groups * dq), lambda b: (b, 0, 0)),
                  pl.BlockSpec((None, dq, LANE), lambda b: (b, 0, 0)),
                  pl.BlockSpec((rows, 1), lambda b: (0, 0))],
        out_specs=pl.BlockSpec((None, rows, LANE), lambda b: (b, 0, 0)),
        compiler_params=_cparams(("parallel",)), name='moba_gate_rows')(qf, kmean_t, qpos)


def _cmp_stage1_body(pt_ref, *refs, n_pp, n_steps, n_sub, ts):
    page_refs = refs[:n_pp]
    new_ref, w_ref, o_ref, x_sc = refs[n_pp:]
    step = pl.program_id(1)
    for pg in range(n_pp):
        off = pl.multiple_of((step * n_pp + pg) * PAGE_SIZE, PAGE_SIZE)
        for c in range(2):
            rows_t = page_refs[pg][c].T
            for hf in range(KVW // LANE):
                x_sc[2 * c + hf, pl.ds(off, PAGE_SIZE), :] = rows_t[:, hf * LANE:(hf + 1) * LANE]

    @pl.when(step == n_steps - 1)
    def _():
        past = n_pp * n_steps * PAGE_SIZE
        tail = n_sub * CMP_STRIDE - past
        for c in range(2):
            for hf in range(KVW // LANE):
                x_sc[2 * c + hf, pl.ds(past, ts), :] = new_ref[:, c * KVW + hf * LANE:c * KVW + (hf + 1) * LANE]
                x_sc[2 * c + hf, pl.ds(past + ts, tail - ts), :] = jnp.zeros((tail - ts, LANE), F32)
            acc = None
            for p in range(CMP_STRIDE):
                xp = jnp.concatenate([x_sc[2 * c + hf, pl.ds(p, n_sub, stride=CMP_STRIDE), :]
                                      for hf in range(KVW // LANE)], axis=1).astype(BF16)
                term = _nn(xp, w_ref[c, p])
                acc = term if acc is None else acc + term
            o_ref[:, c * 2 * KVW:(c + 1) * 2 * KVW] = acc


def _cmp_stage1(cmp_pages, page_table, j, cmp_new, phi_w1, *, n_pp, n_steps, n_sub):
    db, ts, _ = cmp_new.shape
    w1r = phi_w1.reshape(2, 2, CMP_STRIDE, HEAD_DIM, CMP_HIDDEN)
    w = jnp.einsum('chpdx,kl->cpkdhlx', w1r, jnp.eye(N_KV_HEADS, dtype=F32))
    w = w.reshape(2, CMP_STRIDE, KVW, 2 * KVW).astype(BF16)
    in_specs = [pl.BlockSpec((None, None) + cmp_pages.shape[2:],
                             lambda b, s, pt, pg=pg: (pt[b * (n_pp * n_steps) + s * n_pp + pg], j, 0, 0, 0))
                for pg in range(n_pp)]
    in_specs += [pl.BlockSpec((None, ts, 2 * KVW), lambda b, s, pt: (b, 0, 0)),
                 pl.BlockSpec(w.shape, lambda b, s, pt: (0, 0, 0, 0))]
    grid_spec = pltpu.PrefetchScalarGridSpec(
        num_scalar_prefetch=1, grid=(db, n_steps), in_specs=in_specs,
        out_specs=pl.BlockSpec((None, n_sub, 4 * KVW), lambda b, s, pt: (b, 0, 0)),
        scratch_shapes=[pltpu.VMEM((2 * KVW // LANE, n_sub * CMP_STRIDE, LANE), F32)])
    return pl.pallas_call(
        functools.partial(_cmp_stage1_body, n_pp=n_pp, n_steps=n_steps, n_sub=n_sub, ts=ts),
        out_shape=jax.ShapeDtypeStruct((db, n_sub, 4 * KVW), F32), grid_spec=grid_spec,
        compiler_params=_cparams(("parallel", "arbitrary")), name='nsa_compress_pages')(
            page_table.reshape(-1), *([cmp_pages] * n_pp), cmp_new, w)


def _sample_lane_gates(g_t, db, ts, rp):
    g = g_t[:3 * N_HEADS].reshape(3, N_HEADS, db, ts)
    g = jnp.transpose(g, (0, 2, 1, 3)).reshape(3, db, N_HEADS * ts)
    return jnp.pad(g, ((0, 0), (0, 0), (0, rp - N_HEADS * ts))).reshape(3, db, 1, 1, 1, rp)


def _sample_trunk(x3, p, caches, page_table, slopes):
    db, ts, _ = x3.shape
    n = db * ts
    past = page_table.shape[1] * PAGE_SIZE
    x = x3.reshape(n, D_MODEL)
    tm = _row_tile(n)
    rp = LANE
    assert N_HEADS * ts <= rp and ts % SUBLANE == 0
    zero = lambda g: 0
    qpos = _qpos_table(past, ts, ts, N_HEADS, rp)
    slope_tab = _lane_table(slopes.reshape(1, N_HEADS), ts, rp)
    ckv_rows, kpe_rows, moba_rows, cmp_rows, sel_rows, win_states = [], [], [], [], [], []
    pos_rows = past + np.arange(max(tm, ts)) % ts
    kvs = (2, N_KV_HEADS, HEAD_DIM)
    n_pages = page_table.shape[1]
    n_pp = PAGES_PER_STEP
    assert n_pages % n_pp == 0 and past % MOBA_BLOCK == 0
    n_steps = n_pages // n_pp
    rows = N_HEADS * ts
    qpos_col = jnp.asarray((past + np.arange(rows) % ts).astype(np.int32).reshape(rows, 1))
    qpos_mem_col = qpos_col[:N_MEM_HEADS * ts]
    slope_col = jnp.asarray(np.repeat(slopes, ts).astype(np.float32).reshape(rows, 1))

    def feature_major(c):
        nd = c.ndim
        c = jnp.transpose(c, tuple(range(nd - 4)) + (nd - 3, nd - 2, nd - 1, nd - 4))
        return c.reshape(c.shape[:nd - 4] + (2, KVW, c.shape[-1]))

    def new_tiles(kv_new):
        t = jnp.transpose(kv_new.reshape(db, ts, 2, KVW), (0, 2, 3, 1))
        t = jnp.pad(t, ((0, 0), (0, 0), (0, 0), (0, PAGE_SIZE - ts)))
        return t[:, 0], t[:, 1]

    def row_gates(g_t):
        g = g_t[:3 * N_HEADS].reshape(3, N_HEADS, db, ts)
        return jnp.transpose(g, (0, 2, 1, 3)).reshape(3, db, rows, 1)

    def step_sel(sel_rows_, n_cols):
        s = sel_rows_[:, :, :n_steps * n_cols].reshape(db, rows, n_steps, n_cols)
        return jnp.transpose(s, (0, 2, 1, 3))

    moba_pages = feature_major(caches['moba_kv'])
    cmp_pages = feature_major(caches['nsa_cmp'])
    sel_pages = feature_major(caches['nsa_sel'])
    win_ft = feature_major(caches['nsa_win'])
    mem_ft = feature_major(caches['mem_kv'])
    kpe_pages = jnp.swapaxes(caches['mla_kpe'], 2, 3)
    for li in range(DEPTH):
        kind, j = li % 3, li // 3
        w_in = (p['w_in_a'], p['w_in_b'], p['w_in_c'])[kind][j]
        wmem_pad = _pad_head_cols(w_in[:, -MEMQ:], _KV_OF_MEM, _SLOT_S).astype(BF16)
        w_out = p['w_out'][li]
        if kind == 0:
            consts = _mla_weights(w_in, p['q_norm_a'][j], p['kv_norm_a'][j], p['w_uq_a'][j], p['w_uk_a'][j],
                                  p['w_uv_a'][j], wmem_pad, False)
            ckv, kpe, q, qmem = _inproj_mla(x, pos_rows, consts, tm, False, N_MEM_HEADS * _SLOT_S)
            ckv_rows.append(ckv.reshape(db, ts, KV_LORA))
            kpe_rows.append(kpe.reshape(db, ts, QK_ROPE))
            slot_a = KV_LORA + LANE
            w_abs = jnp.zeros((N_HEADS, LANE, N_HEADS, slot_a), F32)
            eye = jnp.eye(QK_ROPE, dtype=F32)
            for h in range(N_HEADS):
                w_abs = w_abs.at[h, :QK_NOPE, h, :KV_LORA].set(p['w_uk_a'][j][:, h, :].T)
                w_abs = w_abs.at[h, QK_NOPE:QK_NOPE + QK_ROPE, h, KV_LORA:KV_LORA + QK_ROPE].set(eye)
            q_abs = _mm(q, w_abs.reshape(N_HEADS * LANE, N_HEADS * slot_a).astype(BF16), tm=tm, tn=slot_a,
                        tk=N_HEADS * LANE, out_dtype=BF16, name='mla_absorb_q')
            ckv_new = jnp.pad(ckv.reshape(db, ts, KV_LORA), ((0, 0), (0, PAGE_SIZE - ts), (0, 0)))
            kpe_new = jnp.pad(jnp.swapaxes(kpe.reshape(db, ts, QK_ROPE), 1, 2), ((0, 0), (0, 0), (0, PAGE_SIZE - ts)))
            o_lat = _decode(q_abs.reshape(db, ts, -1), page_table, caches['mla_ckv'], None, mode='mla', j=j,
                            n_groups=N_HEADS, dq=slot_a, dv=KV_LORA, n_pp=n_pp, pw=PAGE_SIZE, n_steps=n_steps,
                            paged=True, k2_pages=kpe_pages, new=(ckv_new, kpe_new), qpos=qpos_col, past=past,
                            out_dtype=BF16, name='mla_decode')
            w_uv_bd = jnp.einsum('chv,hg->hcgv', p['w_uv_a'][j], jnp.eye(N_HEADS, dtype=F32)).reshape(
                N_HEADS * KV_LORA, HQ)
            o_mix = _mm(o_lat.reshape(n, -1), w_uv_bd.astype(BF16), tm=tm, tn=HQ, tk=N_HEADS * KV_LORA,
                        out_dtype=BF16, name='mla_value_up')
            o_sets = [([o_mix], w_out[:HQ].astype(BF16))]
        elif kind == 1:
            consts = [_pad_head_cols(w_in[:, :HQ], _KV_OF_HEAD, _SLOT_S).astype(BF16),
                      w_in[:, HQ:HQ + 2 * KVW].astype(BF16), wmem_pad]
            plan = [('mm', 0, None, HEAD_DIM ** -0.5), ('prev', 0, None, 1.0), ('mm', 1, None, 1.0),
                    ('mm', 2, None, HEAD_DIM ** -0.5)]
            outs = [(N_HEADS * _SLOT_S, BF16, False), (N_HEADS * _SLOT_S, F32, False), (2 * KVW, F32, False),
                    (N_MEM_HEADS * _SLOT_S, BF16, False)]
            q, qf, kv, qmem = _inproj(x, consts, plan, outs, tm, 'inproj_moba_s')
            moba_rows.append(kv.reshape((db, ts) + kvs))
            n_mean = past // MOBA_BLOCK
            assert n_mean <= LANE
            kmean_t = _kmean_pages(moba_pages, page_table, j, n_pp, n_steps)
            selb = _gate_rows(qf.reshape(db, ts, -1), kmean_t, qpos_col, n_groups=N_HEADS, dq=_SLOT_S, n_cand=n_mean)
            o_mix = _decode(q.reshape(db, ts, -1), page_table, moba_pages, moba_pages, mode='ft', j=j,
                            n_groups=N_HEADS, dq=_SLOT_S, dv=_SLOT_S, n_pp=n_pp, pw=PAGE_SIZE, n_steps=n_steps,
                            paged=True, new=new_tiles(kv), sel=step_sel(selb, n_pp * PAGE_SIZE // MOBA_BLOCK),
                            sel_bs=MOBA_BLOCK, qpos=qpos_col, slopes=slope_col, past=past, out_dtype=BF16,
                            name='moba_decode')
            o_sets = [([o_mix.reshape(n, -1)], _pad_head_rows(w_out[:HQ], _KV_OF_HEAD, _SLOT_S).astype(BF16))]
        else:
            c0 = HQ
            wg_t = jnp.zeros((40, D_MODEL), F32).at[:3 * N_HEADS].set(w_in[:, c0 + 6 * KVW:c0 + 6 * KVW + 3 * N_HEADS].T)
            bg = jnp.zeros((40, 1), F32).at[:3 * N_HEADS, 0].set(p['b_gate_c'][j])
            consts = [_pad_head_cols(w_in[:, :HQ], _KV_OF_HEAD, _SLOT_S).astype(BF16),
                      w_in[:, c0:c0 + 2 * KVW].astype(BF16), w_in[:, c0 + 2 * KVW:c0 + 4 * KVW].astype(BF16),
                      w_in[:, c0 + 4 * KVW:c0 + 6 * KVW].astype(BF16), wg_t.astype(BF16), bg, wmem_pad]
            plan = [('mm', 0, None, HEAD_DIM ** -0.5), ('mm', 1, None, 1.0), ('mm', 2, None, 1.0), ('mm', 3, None, 1.0),
                    ('gateT', 4, 5, 1.0), ('mm', 6, None, HEAD_DIM ** -0.5)]
            outs = [(N_HEADS * _SLOT_S, BF16, False), (2 * KVW, F32, False), (2 * KVW, F32, False),
                    (2 * KVW, F32, False), (40, F32, True), (N_MEM_HEADS * _SLOT_S, BF16, False)]
            q, cmp_kv, sel_kv, win_kv, g_t, qmem = _inproj(x, consts, plan, outs, tm, 'inproj_nsa_s')
            cmp_rows.append(cmp_kv.reshape((db, ts) + kvs))
            sel_rows.append(sel_kv.reshape((db, ts) + kvs))
            wlen = win_ft.shape[-1]
            keep = min(WINDOW, past + ts)
            wk_new, wv_new = new_tiles(win_kv)
            win_seq_ft = jnp.concatenate([win_ft[j], jnp.stack([wk_new, wv_new], 1)[..., :ts]], -1)[..., wlen + ts - keep:]
            win_states.append(jnp.transpose(win_seq_ft.reshape((db,) + kvs + (keep,)), (0, 4, 1, 2, 3)))
            gates = _sample_lane_gates(g_t, db, ts, rp)
            gate_rows = row_gates(g_t)
            q3 = q.reshape(db, ts, -1)
            length = past + ts
            n_cmp = -(-length // CMP_STRIDE) - 1
            n_sub = _round_up(n_cmp + 1, SUBLANE)
            phi = (p['cmp_pe_c'][j], p['phi_w1_c'][j], p['phi_b1_c'][j], p['phi_w2_c'][j], p['phi_b2_c'][j])
            fs = _cmp_stage1(cmp_pages, page_table, j, cmp_kv.reshape(db, ts, -1), phi[1], n_pp=n_pp // 2,
                             n_steps=n_steps * 2, n_sub=n_sub)
            kc, vct = _compress_stage2(fs, *phi)
            n_slc = -(-length // SEL_BLOCK)
            group_of_lane = np.full((rp,), -1, np.int64)
            lanes = np.arange(N_HEADS * ts)
            group_of_lane[:N_HEADS * ts] = (lanes // ts // GQA) * ts + lanes % ts
            o_cmp, sel = _nsa_cmp(q3, kc, vct, qpos, slope_tab, gates[0], n_groups=N_HEADS, dq=_SLOT_S, dk=_SLOT_S,
                                  dv=_SLOT_S, tq=ts, kblk=zero, n_cmp=n_cmp, n_slc=n_slc, group_of_lane=group_of_lane)
            assert past % SEL_BLOCK == 0
            sel_by_row = jnp.swapaxes(sel[:, 0, 0], 1, 2)[:, :rows, :]
            o_sel = _decode(q3, page_table, sel_pages, sel_pages, mode='ft', j=j, n_groups=N_HEADS, dq=_SLOT_S,
                            dv=_SLOT_S, n_pp=n_pp, pw=PAGE_SIZE, n_steps=n_steps, paged=True, new=new_tiles(sel_kv),
                            sel=step_sel(sel_by_row, n_pp * PAGE_SIZE // SEL_BLOCK), sel_bs=SEL_BLOCK,
                            gates=gate_rows[1], qpos=qpos_col, slopes=slope_col, past=past, out_dtype=F32,
                            name='nsa_sel_decode')
            o_win = _decode(q3, page_table, win_ft, win_ft, mode='ft', j=j, n_groups=N_HEADS, dq=_SLOT_S, dv=_SLOT_S,
                            n_pp=1, pw=wlen, n_steps=1, paged=False, new=(wk_new, wv_new), gates=gate_rows[2],
                            qpos=qpos_col, slopes=slope_col, window=WINDOW, kpos0=past - wlen, past=past,
                            out_dtype=F32, name='nsa_win_decode')
            o_sets = [([o.reshape(n, -1) for o in (o_cmp, o_sel, o_win)],
                       _pad_head_rows(w_out[:HQ], _KV_OF_HEAD, _SLOT_S).astype(BF16))]
        o_mem = _decode(qmem.reshape(db, ts, -1), page_table, mem_ft, mem_ft, mode='ft', j=li, n_groups=N_MEM_HEADS,
                        dq=_SLOT_S, dv=_SLOT_S, n_pp=1, pw=N_MEM, n_steps=1, paged=False, qpos=qpos_mem_col,
                        out_dtype=BF16, name='mem_decode')
        o_sets.append(([o_mem.reshape(n, -1)], _pad_head_rows(w_out[HQ:], _KV_OF_MEM, _SLOT_S).astype(BF16)))
        x1, x1b, e_t, g_w = _outproj(x, o_sets, p['ln1_g'][li], p['ln1_b'][li], p['w_router'], p['router_bias'], tm)
        x = _moe(x1, x1b, e_t, g_w, p['w_e_gate'][li], p['w_e_up'][li], p['w_e_down'][li],
                 p['ln2_g'][li], p['ln2_b'][li], tm)
    return (x.reshape(db, ts, D_MODEL), jnp.stack(ckv_rows, 1), jnp.stack(kpe_rows, 1), jnp.stack(moba_rows, 1),
            jnp.stack(cmp_rows, 1), jnp.stack(sel_rows, 1), jnp.stack(win_states, 0))


def kernel(x_prompt, x_sample, cache_mla_ckv, cache_mla_kpe, cache_moba_kv, cache_nsa_cmp_kv, cache_nsa_sel_kv,
           state_nsa_win_kv, cache_mem_kv, page_table, mem_prompt, w_in_a, q_norm_a, kv_norm_a, w_uq_a, w_uk_a,
           w_uv_a, w_in_b, w_in_c, b_gate_c, cmp_pe_c, phi_w1_c, phi_b1_c, phi_w2_c, phi_b2_c, w_mem_kv, w_out,
           ln1_g, ln1_b, ln2_g, ln2_b, w_router, router_bias, w_e_gate, w_e_up, w_e_down):
    p = dict(w_in_a=w_in_a, q_norm_a=q_norm_a, kv_norm_a=kv_norm_a, w_uq_a=w_uq_a, w_uk_a=w_uk_a, w_uv_a=w_uv_a,
             w_in_b=w_in_b, w_in_c=w_in_c, b_gate_c=b_gate_c, cmp_pe_c=cmp_pe_c, phi_w1_c=phi_w1_c,
             phi_b1_c=phi_b1_c, phi_w2_c=phi_w2_c, phi_b2_c=phi_b2_c, w_mem_kv=w_mem_kv, w_out=w_out, ln1_g=ln1_g,
             ln1_b=ln1_b, ln2_g=ln2_g, ln2_b=ln2_b, w_router=w_router, router_bias=router_bias, w_e_gate=w_e_gate,
             w_e_up=w_e_up, w_e_down=w_e_down)
    caches = dict(mla_ckv=cache_mla_ckv, mla_kpe=cache_mla_kpe, moba_kv=cache_moba_kv, nsa_cmp=cache_nsa_cmp_kv,
                  nsa_sel=cache_nsa_sel_kv, nsa_win=state_nsa_win_kv, mem_kv=cache_mem_kv)
    slopes = _alibi_slopes(N_HEADS)
    y_p, p_ckv, p_kpe, p_moba, p_cmp, p_sel, p_win, p_mem = _prompt_trunk(x_prompt, mem_prompt, p, slopes)
    y_s, s_ckv, s_kpe, s_moba, s_cmp, s_sel, s_win = _sample_trunk(x_sample, p, caches, page_table, slopes)
    return (y_p, y_s, p_ckv, p_kpe, p_moba, p_cmp, p_sel, p_win, p_mem,
            s_ckv, s_kpe, s_moba, s_cmp, s_sel, s_win)
```

```python
import functools
import math

import numpy as np
import jax
import jax.numpy as jnp
from jax import lax
from jax.experimental import pallas as pl
from jax.experimental.pallas import tpu as pltpu

F32 = jnp.float32
BF16 = jnp.bfloat16
I32 = jnp.int32

D_MODEL = 1024
DEPTH = 4
PAGE_SIZE = 128
N_HEADS = 12
HEAD_DIM = 64
N_KV_HEADS = 4
GQA = N_HEADS // N_KV_HEADS
HQ = N_HEADS * HEAD_DIM
KVW = N_KV_HEADS * HEAD_DIM
N_MEM = 256
N_MEM_HEADS = 4
MEMQ = N_MEM_HEADS * HEAD_DIM
Q_LORA = 768
KV_LORA = 256
QK_NOPE = 64
QK_ROPE = 32
ROPE_THETA = 10000.0
MOBA_BLOCK = 256
MOBA_TOPK = 3
CMP_BLOCK = 32
CMP_STRIDE = 16
CMP_HIDDEN = 64
SEL_BLOCK = 64
N_SEL = 16
WINDOW = 512
N_EXPERTS = 16
N_GROUPS = 4
EXPERTS_PER_GROUP = N_EXPERTS // N_GROUPS
TOP_K = 2
D_EXPERT = 512
ALPHA = (2 * DEPTH) ** 0.25
EPS = 1e-5
NEG = -1e30
FORCE = 1e9
TINY = 1e-30

LANE = 128
SUBLANE = 8
VMEM_LIMIT = 56 * 1024 * 1024
MOE_TILE = 256
HIGHEST = lax.Precision.HIGHEST


def _cparams(sem):
    return pltpu.CompilerParams(dimension_semantics=sem, vmem_limit_bytes=VMEM_LIMIT)


def _alibi_slopes(n):
    def pow2(m):
        start = 2.0 ** (-8.0 / m)
        return [start ** (i + 1) for i in range(m)]
    if n & (n - 1) == 0:
        s = pow2(n)
    else:
        c = 2 ** int(np.floor(np.log2(n)))
        s = pow2(c) + pow2(2 * c)[0::2][:n - c]
    return np.array(s, np.float32)


def _nt(a, b, precision=None):
    return lax.dot_general(a, b, (((1,), (1,)), ((), ())), preferred_element_type=F32, precision=precision)


def _nn(a, b, precision=None):
    return jnp.dot(a, b, preferred_element_type=F32, precision=precision)


def _round_up(x, m):
    return -(-x // m) * m


def _row_call(body, n_rows, tm, ins, outs, name):
    assert n_rows % tm == 0
    n_tiles = n_rows // tm
    in_specs, arrays = [], []
    for arr, kind in ins:
        arrays.append(arr)
        if kind == 'row':
            in_specs.append(pl.BlockSpec((tm, arr.shape[1]), lambda i: (i, 0)))
        elif kind == 'rowT':
            in_specs.append(pl.BlockSpec((arr.shape[0], tm), lambda i: (0, i)))
        elif kind == 'period':
            per = arr.shape[0] // tm
            assert arr.shape[0] % tm == 0
            in_specs.append(pl.BlockSpec((tm, arr.shape[1]), lambda i, per=per: (i % per, 0)))
        else:
            nd = arr.ndim
            in_specs.append(pl.BlockSpec(arr.shape, lambda i, nd=nd: (0,) * nd))
    out_shapes, out_specs = [], []
    for width, dtype, transposed in outs:
        if transposed:
            out_shapes.append(jax.ShapeDtypeStruct((width, n_rows), dtype))
            out_specs.append(pl.BlockSpec((width, tm), lambda i: (0, i)))
        else:
            out_shapes.append(jax.ShapeDtypeStruct((n_rows, width), dtype))
            out_specs.append(pl.BlockSpec((tm, width), lambda i: (i, 0)))
    return pl.pallas_call(
        body, out_shape=out_shapes, grid=(n_tiles,), in_specs=in_specs, out_specs=out_specs,
        compiler_params=_cparams(("parallel",)), name=name)(*arrays)


def _rmsnorm(x, g):
    return x * lax.rsqrt(jnp.mean(x * x, axis=-1, keepdims=True) + EPS) * g


def _layernorm(x, g, b):
    xc = x - jnp.mean(x, axis=-1, keepdims=True)
    var = jnp.mean(xc * xc, axis=-1, keepdims=True)
    return xc * lax.rsqrt(var + EPS) * g + b


def _tile_lanes(x, n):
    return jnp.concatenate([x] * n, axis=1) if n > 1 else x


def _mm_body(x_ref, w_ref, o_ref, acc_ref, *, nk):
    k = pl.program_id(2)

    @pl.when(k == 0)
    def _():
        acc_ref[...] = jnp.zeros_like(acc_ref)

    acc_ref[...] += _nn(x_ref[...].astype(BF16), w_ref[...].astype(BF16))

    @pl.when(k == nk - 1)
    def _():
        o_ref[...] = acc_ref[...].astype(o_ref.dtype)


def _mm(x, w, *, tm, tn, tk, out_dtype, name):
    m, kd = x.shape
    n = w.shape[1]
    assert m % tm == 0 and n % tn == 0 and kd % tk == 0
    nk = kd // tk
    return pl.pallas_call(
        functools.partial(_mm_body, nk=nk),
        out_shape=jax.ShapeDtypeStruct((m, n), out_dtype),
        grid=(m // tm, n // tn, nk),
        in_specs=[pl.BlockSpec((tm, tk), lambda i, j, k: (i, k)),
                  pl.BlockSpec((tk, tn), lambda i, j, k: (k, j))],
        out_specs=pl.BlockSpec((tm, tn), lambda i, j, k: (i, j)),
        scratch_shapes=[pltpu.VMEM((tm, tn), F32)],
        compiler_params=_cparams(("parallel", "parallel", "arbitrary")), name=name)(x, w)


def _stack_rows(q_ref, n_groups, dq, rp):
    pieces = [q_ref[:, g * dq:(g + 1) * dq] for g in range(n_groups)]
    rows = sum(p.shape[0] for p in pieces)
    if rows < rp:
        pieces.append(jnp.zeros((rp - rows, dq), pieces[0].dtype))
    return jnp.concatenate(pieces, axis=0) if len(pieces) > 1 else pieces[0]


def _unstack_rows(o, n_groups, tq):
    pieces = [o[g * tq:(g + 1) * tq, :] for g in range(n_groups)]
    return jnp.concatenate(pieces, axis=1) if n_groups > 1 else pieces[0]


def _flash_body(*refs, cfg):
    (n_groups, dq, tq, tk, dv, rp, nkt, has_slopes, has_sel, has_gate, sel_bs, mode, window) = cfg
    refs = list(refs)
    q_ref = refs.pop(0)
    k_ref = refs.pop(0)
    v_ref = refs.pop(0)
    qpos_ref = refs.pop(0)
    slope_ref = refs.pop(0) if has_slopes else None
    sel_ref = refs.pop(0) if has_sel else None
    gate_ref = refs.pop(0) if has_gate else None
    o_ref, m_sc, l_sc, acc_sc = refs
    qi = pl.program_id(2)
    m_sc[...] = jnp.full_like(m_sc, NEG)
    l_sc[...] = jnp.zeros_like(l_sc)
    acc_sc[...] = jnp.zeros_like(acc_sc)
    qs = _stack_rows(q_ref, n_groups, dq, rp).astype(BF16)

    def tile(ki, masked):
        k0 = pl.multiple_of(ki * tk, tk)
        kt = k_ref[pl.ds(k0, tk), :].astype(BF16)
        s = _nt(kt, qs)
        kpos = k0 + lax.broadcasted_iota(I32, (tk, 1), 0)
        if has_slopes:
            s = s + slope_ref[...] * kpos.astype(F32)
        if has_sel:
            nb = tk // sel_bs
            rows = [sel_ref[pl.ds(ki * nb + j, 1), :] for j in range(nb)]
            if nb == 1:
                s = s + rows[0]
            else:
                s = s + jnp.concatenate([jnp.broadcast_to(r, (sel_bs, rp)) for r in rows], axis=0)
        if masked:
            d = qpos_ref[...] - kpos
            mask = d >= 0
            if window is not None:
                mask = mask & (d <= window)
            s = jnp.where(mask, s, NEG)
        m_prev = m_sc[...]
        m_new = jnp.maximum(m_prev, jnp.max(s, axis=0, keepdims=True))
        alpha = jnp.exp(m_prev - m_new)
        p = jnp.exp(s - m_new)
        if masked:
            p = jnp.where(s > 0.5 * NEG, p, 0.0)
        l_sc[...] = alpha * l_sc[...] + jnp.sum(p, axis=0, keepdims=True)
        vt = v_ref[:, pl.ds(k0, tk)].astype(BF16)
        acc_sc[...] = alpha * acc_sc[...] + _nn(vt, p.astype(BF16))
        m_sc[...] = m_new

    if mode == 'causal':
        n_int = (qi * tq) // tk
        lax.fori_loop(0, n_int, lambda ki, c: (tile(ki, False), c)[1], 0)
        for r in range(tq // tk):
            tile(n_int + r, True)
    elif mode == 'window':
        klo = jnp.maximum(qi * tq - window, 0) // tk
        khi = ((qi + 1) * tq - 1) // tk
        lax.fori_loop(klo, khi + 1, lambda ki, c: (tile(ki, True), c)[1], 0)
    else:
        for ki in range(nkt):
            tile(ki, False)

    inv = jnp.where(m_sc[...] > 0.5 * NEG, 1.0 / jnp.maximum(l_sc[...], TINY), 0.0)
    if has_gate:
        inv = inv * gate_ref[...]
    o = (acc_sc[...] * inv).T
    o_ref[...] = _unstack_rows(o, n_groups, tq).astype(o_ref.dtype)


def _flash(q, k, vt, qpos, *, n_groups, dq, dk, dv, tq, tk, kblk, vblk, mode='causal', slopes=None,
           sel=None, sel_bs=None, gates=None, window=None, out_dtype=BF16, name='flash'):
    nb, t_q, wq = q.shape
    t_k = k.shape[1]
    n_g = wq // (n_groups * dq)
    nqt, nkt = t_q // tq, t_k // tk
    rp = qpos.shape[-1]
    assert t_q % tq == 0 and t_k % tk == 0 and tq % tk == 0 and rp >= n_groups * tq and rp % LANE == 0
    in_specs = [pl.BlockSpec((None, tq, n_groups * dq), lambda b, g, qi: (b, qi, g)),
                pl.BlockSpec((None, t_k, dk), lambda b, g, qi: (b, 0, kblk(g))),
                pl.BlockSpec((None, dv, t_k), lambda b, g, qi: (b, vblk(g), 0)),
                pl.BlockSpec((None, 1, rp), lambda b, g, qi: (qi, 0, 0))]
    args = [q, k, vt, qpos]
    if slopes is not None:
        in_specs.append(pl.BlockSpec((None, 1, rp), lambda b, g, qi: (g, 0, 0)))
        args.append(slopes)
    if sel is not None:
        nblk = sel.shape[3]
        in_specs.append(pl.BlockSpec((None, None, None, nblk, rp), lambda b, g, qi: (b, g, qi, 0, 0)))
        args.append(sel)
    if gates is not None:
        in_specs.append(pl.BlockSpec((None, None, None, 1, rp), lambda b, g, qi: (b, g, qi, 0, 0)))
        args.append(gates)
    cfg = (n_groups, dq, tq, tk, dv, rp, nkt, slopes is not None, sel is not None, gates is not None, sel_bs,
           mode, window)
    return pl.pallas_call(
        functools.partial(_flash_body, cfg=cfg),
        out_shape=jax.ShapeDtypeStruct((nb, t_q, n_g * n_groups * dv), out_dtype),
        grid=(nb, n_g, nqt), in_specs=in_specs,
        out_specs=pl.BlockSpec((None, tq, n_groups * dv), lambda b, g, qi: (b, qi, g)),
        scratch_shapes=[pltpu.VMEM((1, rp), F32), pltpu.VMEM((1, rp), F32), pltpu.VMEM((dv, rp), F32)],
        compiler_params=_cparams(("parallel", "parallel", "parallel")), name=name)(*args)


def _qpos_table(q_start, t_q, tq, n_groups, rp):
    nqt = t_q // tq
    base = q_start + np.arange(nqt)[:, None] * tq + np.tile(np.arange(tq), n_groups)[None, :]
    out = np.full((nqt, 1, rp), -1, np.int32)
    out[:, 0, :n_groups * tq] = base
    return jnp.asarray(out)


def _lane_table(per_group_vals, tq, rp):
    vals = np.asarray(per_group_vals, np.float32)
    n_g, n_groups = vals.shape
    out = np.zeros((n_g, 1, rp), np.float32)
    out[:, 0, :n_groups * tq] = np.repeat(vals, tq, axis=1)
    return jnp.asarray(out)


def _kmean_body(k_ref, o_ref):
    k = k_ref[...].astype(F32)
    w = k.shape[-1]
    o_ref[...] = jnp.sum(k.reshape(SUBLANE, MOBA_BLOCK, w), axis=1) * (1.0 / MOBA_BLOCK)


def _moba_kmean(k, n_mean):
    nb, _, w = k.shape
    assert n_mean % SUBLANE == 0
    return pl.pallas_call(
        _kmean_body, out_shape=jax.ShapeDtypeStruct((nb, n_mean, w), F32), grid=(nb, n_mean // SUBLANE),
        in_specs=[pl.BlockSpec((None, SUBLANE * MOBA_BLOCK, w), lambda b, c: (b, c, 0))],
        out_specs=pl.BlockSpec((None, SUBLANE, w), lambda b, c: (b, c, 0)),
        compiler_params=_cparams(("parallel", "parallel")), name='moba_kmean')(k)


def _rank_desc(vals, n_cand, row_iota):
    rank = jnp.zeros(vals.shape, I32)
    for i in range(n_cand):
        vi = vals[i:i + 1, :]
        rank = rank + ((vi > vals) | ((vi == vals) & (i < row_iota))).astype(I32)
    return rank


def _moba_gate_body(q_ref, km_ref, qpos_ref, o_ref, *, n_groups, dq, rp, n_cand):
    qs = _stack_rows(q_ref, n_groups, dq, rp)
    gate = _nt(km_ref[...], qs, precision=HIGHEST)
    nblk_pad = o_ref.shape[0]
    if nblk_pad > gate.shape[0]:
        gate = jnp.concatenate([gate, jnp.zeros((nblk_pad - gate.shape[0], rp), F32)], axis=0)
    own = qpos_ref[...] // MOBA_BLOCK
    n_iota = lax.broadcasted_iota(I32, gate.shape, 0)
    elig = n_iota < own
    g = jnp.where(elig, gate, NEG)
    rank = _rank_desc(g, n_cand, n_iota)
    selected = (elig & (rank < MOBA_TOPK)) | (n_iota == own)
    o_ref[...] = jnp.where(selected, 0.0, NEG)


def _moba_gate(qf, kmean, qpos, *, n_groups, dq, dk, tq, kblk, n_cand, nblk_pad):
    nb, t_q, wq = qf.shape
    n_g = wq // (n_groups * dq)
    nqt = t_q // tq
    rp = qpos.shape[-1]
    n_mean = kmean.shape[1]
    return pl.pallas_call(
        functools.partial(_moba_gate_body, n_groups=n_groups, dq=dq, rp=rp, n_cand=n_cand),
        out_shape=jax.ShapeDtypeStruct((nb, n_g, nqt, nblk_pad, rp), F32), grid=(nb, n_g, nqt),
        in_specs=[pl.BlockSpec((None, tq, n_groups * dq), lambda b, g, qi: (b, qi, g)),
                  pl.BlockSpec((None, n_mean, dk), lambda b, g, qi: (b, 0, kblk(g))),
                  pl.BlockSpec((None, 1, rp), lambda b, g, qi: (qi, 0, 0))],
        out_specs=pl.BlockSpec((None, None, None, nblk_pad, rp), lambda b, g, qi: (b, g, qi, 0, 0)),
        compiler_params=_cparams(("parallel", "parallel", "parallel")), name='moba_gate')(qf, kmean, qpos)


def _pad_head_cols(w, kv_of_head, slot=LANE):
    d = w.shape[0]
    n_h = len(kv_of_head)
    per = slot // HEAD_DIM
    out = jnp.zeros((d, n_h, per, HEAD_DIM), w.dtype)
    w3 = w.reshape(d, n_h, HEAD_DIM)
    for h, kv in enumerate(kv_of_head):
        out = out.at[:, h, kv % per, :].set(w3[:, h, :])
    return out.reshape(d, n_h * slot)


def _pad_head_rows(w, kv_of_head, slot=LANE):
    return _pad_head_cols(w.T, kv_of_head, slot).T


_KV_OF_HEAD = [h // GQA for h in range(N_HEADS)]
_KV_OF_MEM = list(range(N_MEM_HEADS))


def _moba_prompt_attend(q, qf, kv, vt, slopes, *, tq=256):
    b, t, _ = q.shape
    tk = MOBA_BLOCK
    nblk = t // MOBA_BLOCK
    rp = GQA * tq
    qpos = _qpos_table(0, t, tq, GQA, rp)
    n_mean = _round_up(nblk, SUBLANE)
    kmean = _moba_kmean(kv, n_mean)
    pair = lambda g: g // 2
    sel = _moba_gate(qf, kmean, qpos, n_groups=GQA, dq=LANE, dk=LANE, tq=tq, kblk=pair, n_cand=nblk - 1,
                     nblk_pad=n_mean)
    slope_tab = _lane_table(slopes.reshape(N_KV_HEADS, GQA), tq, rp)
    return _flash(q, kv, vt, qpos, n_groups=GQA, dq=LANE, dk=LANE, dv=LANE, tq=tq, tk=tk, kblk=pair, vblk=pair,
                  slopes=slope_tab, sel=sel, sel_bs=MOBA_BLOCK, out_dtype=BF16, name='moba_flash')


def _compress_weight(phi_w1):
    w1r = phi_w1.reshape(2, 2, CMP_STRIDE, HEAD_DIM, CMP_HIDDEN)
    eye_c = jnp.eye(2, dtype=phi_w1.dtype)
    eye_k = jnp.eye(N_KV_HEADS, dtype=phi_w1.dtype)
    w = jnp.einsum('chpdx,ce,kl->pckdehlx', w1r, eye_c, eye_k)
    return w.reshape(CMP_STRIDE * 2 * KVW, 2 * 2 * N_KV_HEADS * CMP_HIDDEN)


def _block_diag4(w):
    return jnp.einsum('xy,kl->kxly', w, jnp.eye(N_KV_HEADS, dtype=w.dtype)).reshape(KVW, KVW)


def _compress2_body(fs_ref, pe_ref, w1_ref, b1_ref, w2k_ref, w2vt_ref, b2k_ref, b2v_ref, kc_ref, vct_ref):
    fs = fs_ref[...]
    n_sub = fs.shape[0]
    outs = []
    for c in range(2):
        first = fs[:, (2 * c) * KVW:(2 * c + 1) * KVW]
        second = fs[:, (2 * c + 1) * KVW:(2 * c + 2) * KVW]
        second = pltpu.roll(second, shift=n_sub - 1, axis=0)
        c1 = _nn(pe_ref[c], w1_ref[c], precision=HIGHEST)[0:1, :] + b1_ref[c]
        hid = jax.nn.gelu(first + second + _tile_lanes(c1, N_KV_HEADS))
        outs.append(hid.astype(BF16))
    kc_ref[...] = (_nn(outs[0], w2k_ref[...]) + b2k_ref[...]).astype(kc_ref.dtype)
    vct_ref[...] = (_nt(w2vt_ref[...], outs[1]) + b2v_ref[...]).astype(vct_ref.dtype)


def _nsa_compress(cmp_rows, cmp_pe, phi_w1, phi_b1, phi_w2, phi_b2):
    nb, length, _ = cmp_rows.shape
    n_sub = length // CMP_STRIDE
    stacked = cmp_rows.reshape(nb * n_sub, CMP_STRIDE * 2 * KVW)
    m = nb * n_sub
    tm = 512 if m % 512 == 0 else n_sub
    fs = _mm(stacked, _compress_weight(phi_w1).astype(BF16), tm=tm, tn=512, tk=2048, out_dtype=F32,
             name='nsa_compress_mm').reshape(nb, n_sub, 4 * KVW)
    return _compress_stage2(fs, cmp_pe, phi_w1, phi_b1, phi_w2, phi_b2)


def _compress_stage2(fs, cmp_pe, phi_w1, phi_b1, phi_w2, phi_b2):
    nb, n_sub, _ = fs.shape
    pe = jnp.zeros((2, SUBLANE, CMP_BLOCK * HEAD_DIM), F32).at[:, 0, :].set(cmp_pe.reshape(2, -1))
    w2k = _block_diag4(phi_w2[0]).astype(BF16)
    w2vt = _block_diag4(phi_w2[1]).T.astype(BF16)
    b2k = jnp.tile(phi_b2[0], N_KV_HEADS).reshape(1, KVW)
    b2v = jnp.tile(phi_b2[1], N_KV_HEADS).reshape(KVW, 1)
    const = lambda a: pl.BlockSpec(a.shape, lambda b, nd=a.ndim: (0,) * nd)
    b1 = phi_b1.reshape(2, 1, CMP_HIDDEN)
    consts = [pe, phi_w1, b1, w2k, w2vt, b2k, b2v]
    return pl.pallas_call(
        _compress2_body,
        out_shape=[jax.ShapeDtypeStruct((nb, n_sub, KVW), BF16), jax.ShapeDtypeStruct((nb, KVW, n_sub), BF16)],
        grid=(nb,),
        in_specs=[pl.BlockSpec((None, n_sub, 4 * KVW), lambda b: (b, 0, 0))] + [const(a) for a in consts],
        out_specs=[pl.BlockSpec((None, n_sub, KVW), lambda b: (b, 0, 0)),
                   pl.BlockSpec((None, KVW, n_sub), lambda b: (b, 0, 0))],
        compiler_params=_cparams(("parallel",)), name='nsa_compress_mlp')(fs, *consts)


def _nsa_cmp_body(q_ref, kc_ref, vct_ref, qpos_ref, slope_ref, gate_ref, cover_ref, gsum_ref, o_ref, sel_ref, *,
                  n_groups, dq, tq, rp, n_cmp, n_slc, n_top, lane_sum):
    qs = _stack_rows(q_ref, n_groups, dq, rp).astype(BF16)
    s = _nt(kc_ref[...], qs)
    n_sub = s.shape[0]
    n_iota = lax.broadcasted_iota(I32, (n_sub, 1), 0)
    qpos = qpos_ref[...]
    d = qpos - (n_iota * CMP_STRIDE + (CMP_BLOCK - 1))
    s = s - slope_ref[...] * d.astype(F32)
    valid = (d >= 0) & (n_iota < n_cmp)
    s = jnp.where(valid, s, NEG)
    m = jnp.max(s, axis=0, keepdims=True)
    e = jnp.where(valid, jnp.exp(s - m), 0.0)
    p = e / jnp.maximum(jnp.sum(e, axis=0, keepdims=True), TINY)
    o = (_nn(vct_ref[...], p.astype(BF16)) * gate_ref[...]).T
    o_ref[...] = _unstack_rows(o, n_groups, tq).astype(o_ref.dtype)
    imp = _nn(cover_ref[...], p, precision=HIGHEST)
    if lane_sum:
        tot = imp[:, 0:tq]
        for g in range(1, n_groups):
            tot = tot + imp[:, g * tq:(g + 1) * tq]
        imp = _tile_lanes(tot, n_groups)
    else:
        imp = _nn(imp, gsum_ref[...], precision=HIGHEST)
    cur = qpos // SEL_BLOCK
    jj = lax.broadcasted_iota(I32, imp.shape, 0)
    forced = (jj == 0) | (jj == cur) | (jj == cur - 1)
    imp = jnp.where(jj > cur, NEG, jnp.where(forced, FORCE, imp))
    rank = _rank_desc(imp, n_slc, jj)
    sel_ref[...] = jnp.where((rank < n_top) & (jj < n_slc), 0.0, NEG)


def _nsa_cmp(q, kc, vct, qpos, slopes, gates, *, n_groups, dq, dk, dv, tq, kblk, n_cmp, n_slc, group_of_lane):
    nb, t_q, wq = q.shape
    n_g = wq // (n_groups * dq)
    nqt = t_q // tq
    rp = qpos.shape[-1]
    n_sub = kc.shape[1]
    n_slc_pad = _round_up(n_slc, SUBLANE)
    ci = np.arange(n_sub)[None, :] * CMP_STRIDE
    sj = np.arange(n_slc_pad)[:, None] * SEL_BLOCK
    cover_t = ((ci < sj + SEL_BLOCK) & (ci + CMP_BLOCK > sj) & (np.arange(n_sub)[None, :] < n_cmp)
               & (np.arange(n_slc_pad)[:, None] < n_slc)).astype(np.float32)
    lane_sum = tq % LANE == 0
    gl = np.asarray(group_of_lane)
    gsum = ((gl[:, None] == gl[None, :]) & (gl[:, None] >= 0)).astype(np.float32)
    n_top = min(N_SEL, n_slc)
    body = functools.partial(_nsa_cmp_body, n_groups=n_groups, dq=dq, tq=tq, rp=rp, n_cmp=n_cmp, n_slc=n_slc,
                             n_top=n_top, lane_sum=lane_sum)
    gidx = lambda b, g, qi: (b, g, qi, 0, 0)
    return pl.pallas_call(
        body,
        out_shape=[jax.ShapeDtypeStruct((nb, t_q, n_g * n_groups * dv), F32),
                   jax.ShapeDtypeStruct((nb, n_g, nqt, n_slc_pad, rp), F32)],
        grid=(nb, n_g, nqt),
        in_specs=[pl.BlockSpec((None, tq, n_groups * dq), lambda b, g, qi: (b, qi, g)),
                  pl.BlockSpec((None, n_sub, dk), lambda b, g, qi: (b, 0, kblk(g))),
                  pl.BlockSpec((None, dv, n_sub), lambda b, g, qi: (b, kblk(g), 0)),
                  pl.BlockSpec((None, 1, rp), lambda b, g, qi: (qi, 0, 0)),
                  pl.BlockSpec((None, 1, rp), lambda b, g, qi: (g, 0, 0)),
                  pl.BlockSpec((None, None, None, 1, rp), gidx),
                  pl.BlockSpec((n_slc_pad, n_sub), lambda b, g, qi: (0, 0)),
                  pl.BlockSpec((rp, rp), lambda b, g, qi: (0, 0))],
        out_specs=[pl.BlockSpec((None, tq, n_groups * dv), lambda b, g, qi: (b, qi, g)),
                   pl.BlockSpec((None, None, None, n_slc_pad, rp), gidx)],
        compiler_params=_cparams(("parallel", "parallel", "parallel")), name='nsa_cmp')(
            q, kc, vct, qpos, slopes, gates, jnp.asarray(cover_t), jnp.asarray(gsum))


def _gate_lanes(g_t, b, t, tq):
    nqt = t // tq
    g = g_t[:3 * N_HEADS].reshape(3, N_KV_HEADS, GQA, b, nqt, tq)
    return jnp.transpose(g, (0, 3, 1, 4, 2, 5)).reshape(3, b, N_KV_HEADS, nqt, 1, GQA * tq)


def _nsa_prompt_attend(q, cmp_kv, sel_kv, win_kv, vt_sel, vt_win, gates, slopes, phi, *, tq=256):
    b, t, _ = q.shape
    rp = GQA * tq
    qpos = _qpos_table(0, t, tq, GQA, rp)
    slope_tab = _lane_table(slopes.reshape(N_KV_HEADS, GQA), tq, rp)
    pair = lambda g: g // 2
    kc, vct = _nsa_compress(cmp_kv, *phi)
    n_cmp = t // CMP_STRIDE - 1
    n_slc = t // SEL_BLOCK
    group_of_lane = np.tile(np.arange(tq), GQA)
    o_cmp, sel = _nsa_cmp(q, kc, vct, qpos, slope_tab, gates[0], n_groups=GQA, dq=LANE, dk=LANE, dv=LANE, tq=tq,
                          kblk=pair, n_cmp=n_cmp, n_slc=n_slc, group_of_lane=group_of_lane)
    tk = 256
    o_sel = _flash(q, sel_kv, vt_sel, qpos, n_groups=GQA, dq=LANE, dk=LANE, dv=LANE, tq=tq, tk=tk, kblk=pair,
                   vblk=pair, slopes=slope_tab, sel=sel, sel_bs=SEL_BLOCK, gates=gates[1],
                   out_dtype=F32, name='nsa_sel_flash')
    o_win = _flash(q, win_kv, vt_win, qpos, n_groups=GQA, dq=LANE, dk=LANE, dv=LANE, tq=tq, tk=tk, kblk=pair,
                   vblk=pair, mode='window', slopes=slope_tab, gates=gates[2], window=WINDOW,
                   out_dtype=F32, name='nsa_win_flash')
    return o_cmp, o_sel, o_win


def _inproj_body(*refs, plan, n_const):
    x_ref = refs[0]
    consts = refs[1:1 + n_const]
    outs = refs[1 + n_const:]
    xb = x_ref[...].astype(BF16)
    y = None
    for (kind, wi, bi, scale), o_ref in zip(plan, outs):
        if kind == 'mm':
            y = _nn(xb, consts[wi][...])
            if scale != 1.0:
                y = y * scale
        elif kind == 'mmT':
            y = _nt(consts[wi][...], xb)
        elif kind == 'gateT':
            y = jax.nn.sigmoid(_nt(consts[wi][...], xb) + consts[bi][...])
        o_ref[...] = y.astype(o_ref.dtype)


def _inproj(x, consts, plan, outs, tm, name):
    body = functools.partial(_inproj_body, plan=plan, n_const=len(consts))
    return _row_call(body, x.shape[0], tm, [(x, 'row')] + [(c, 'const') for c in consts], outs, name)


def _rope_tables(pos):
    half = QK_ROPE // 2
    inv = ROPE_THETA ** (-np.arange(half, dtype=np.float32) / half)
    ang = pos.astype(np.float32)[:, None] * inv[None, :]
    cos = np.concatenate([np.cos(ang), np.cos(ang)], -1).astype(np.float32)
    sin = np.concatenate([np.sin(ang), np.sin(ang)], -1).astype(np.float32)
    n = pos.shape[0]
    ctab = np.concatenate([np.ones((n, QK_NOPE), np.float32), cos, np.zeros((n, LANE - QK_NOPE - QK_ROPE), np.float32)], -1)
    stab = np.concatenate([np.zeros((n, QK_NOPE), np.float32), sin, np.zeros((n, LANE - QK_NOPE - QK_ROPE), np.float32)], -1)
    return jnp.asarray(cos), jnp.asarray(sin), jnp.asarray(ctab), jnp.asarray(stab)


def _rot_cols(w):
    half = QK_ROPE // 2
    return jnp.concatenate([-w[:, half:], w[:, :half]], axis=1)


def _inproj_mla_body(x_ref, cos_ref, sin_ref, ctab_ref, stab_ref, wcq, wckv, wpe, wper, wmem, qn, kvn, wuq, wuqr,
                     *rest, prompt, scale):
    if prompt:
        wuk, epe, wuvt, ckv_o, kpe_o, q_o, qmem_o, kh_o, vht_o = rest
    else:
        ckv_o, kpe_o, q_o, qmem_o = rest
    xb = x_ref[...].astype(BF16)
    cq = _rmsnorm(_nn(xb, wcq[...]), qn[...])
    ckv = _rmsnorm(_nn(xb, wckv[...]), kvn[...])
    kpe = _nn(xb, wpe[...]) * cos_ref[...] + _nn(xb, wper[...]) * sin_ref[...]
    ckv_o[...] = ckv
    kpe_o[...] = kpe
    cqb = cq.astype(BF16)
    q = _nn(cqb, wuq[...]) * _tile_lanes(ctab_ref[...], N_HEADS) + _nn(cqb, wuqr[...]) * _tile_lanes(stab_ref[...], N_HEADS)
    q_o[...] = (q * scale).astype(q_o.dtype)
    qmem_o[...] = (_nn(xb, wmem[...]) * (HEAD_DIM ** -0.5)).astype(qmem_o.dtype)
    if prompt:
        ckvb = ckv.astype(BF16)
        kh_o[...] = (_nn(ckvb, wuk[...]) + _nn(kpe.astype(BF16), epe[...])).astype(kh_o.dtype)
        vht_o[...] = _nt(wuvt[...], ckvb).astype(vht_o.dtype)


def _mla_weights(w_in, q_norm, kv_norm, w_uq, w_uk, w_uv, wmem_pad, prompt):
    e = QK_NOPE + QK_ROPE
    wcq = w_in[:, :Q_LORA].astype(BF16)
    wckv = w_in[:, Q_LORA:Q_LORA + KV_LORA].astype(BF16)
    wpe_f = w_in[:, Q_LORA + KV_LORA:Q_LORA + KV_LORA + QK_ROPE]
    pad = jnp.zeros((Q_LORA, N_HEADS, LANE - e), F32)
    wuq = jnp.concatenate([w_uq, pad], -1).reshape(Q_LORA, N_HEADS * LANE).astype(BF16)
    rot = jnp.concatenate([jnp.zeros((Q_LORA, N_HEADS, QK_NOPE), F32),
                           jax.vmap(_rot_cols, in_axes=1, out_axes=1)(w_uq[:, :, QK_NOPE:]), pad], -1)
    wuqr = rot.reshape(Q_LORA, N_HEADS * LANE).astype(BF16)
    consts = [wcq, wckv, wpe_f.astype(BF16), _rot_cols(wpe_f).astype(BF16), wmem_pad,
              q_norm.reshape(1, -1), kv_norm.reshape(1, -1), wuq, wuqr]
    if prompt:
        wuk = jnp.concatenate([w_uk, jnp.zeros((KV_LORA, N_HEADS, LANE - QK_NOPE), F32)], -1)
        epe = jnp.zeros((QK_ROPE, N_HEADS, LANE), F32).at[:, :, QK_NOPE:e].set(
            jnp.broadcast_to(jnp.eye(QK_ROPE, dtype=F32)[:, None, :], (QK_ROPE, N_HEADS, QK_ROPE)))
        wuvt = w_uv.reshape(KV_LORA, HQ).T
        consts += [wuk.reshape(KV_LORA, N_HEADS * LANE).astype(BF16), epe.reshape(QK_ROPE, N_HEADS * LANE).astype(BF16),
                   wuvt.astype(BF16)]
    return consts


def _inproj_mla(x, pos_rows, consts, tm, prompt, mem_w):
    cos, sin, ctab, stab = _rope_tables(pos_rows)
    n = x.shape[0]
    outs = [(KV_LORA, F32, False), (QK_ROPE, F32, False), (N_HEADS * LANE, BF16, False), (mem_w, BF16, False)]
    if prompt:
        outs += [(N_HEADS * LANE, BF16, False), (HQ, BF16, True)]
    body = functools.partial(_inproj_mla_body, prompt=prompt, scale=(QK_NOPE + QK_ROPE) ** -0.5)
    ins = [(x, 'row'), (cos, 'period'), (sin, 'period'), (ctab, 'period'), (stab, 'period')] + [(c, 'const') for c in consts]
    return _row_call(body, n, tm, ins, outs, 'inproj_mla')


def _outproj_body(*refs, set_sizes):
    n_o = sum(set_sizes)
    n_sets = len(set_sizes)
    x_ref = refs[0]
    o_refs = refs[1:1 + n_o]
    w_refs = refs[1 + n_o:1 + n_o + n_sets]
    g_ref, b_ref, wr_ref, rb_ref, x1_o, x1b_o, e_o, gw_o = refs[1 + n_o + n_sets:]
    acc = None
    idx = 0
    for si, n in enumerate(set_sizes):
        a = o_refs[idx][...]
        for j in range(1, n):
            a = a + o_refs[idx + j][...]
        idx += n
        term = _nn(a.astype(BF16), w_refs[si][...])
        acc = term if acc is None else acc + term
    x1 = _layernorm(ALPHA * x_ref[...] + acc, g_ref[...], b_ref[...])
    x1_o[...] = x1
    x1b_o[...] = x1.astype(BF16)
    scores = jax.nn.sigmoid(_nt(wr_ref[...], x1, precision=HIGHEST))
    biased = scores + rb_ref[...]
    rb = [biased[i:i + 1, :] for i in range(N_EXPERTS)]
    rs = [scores[i:i + 1, :] for i in range(N_EXPERTS)]
    gbest, gidx = None, None
    for g in range(N_GROUPS):
        v = rb[g * EXPERTS_PER_GROUP:(g + 1) * EXPERTS_PER_GROUP]
        top2 = None
        for i in range(EXPERTS_PER_GROUP):
            for j in range(i + 1, EXPERTS_PER_GROUP):
                pair = v[i] + v[j]
                top2 = pair if top2 is None else jnp.maximum(top2, pair)
        if g == 0:
            gbest, gidx = top2, jnp.zeros(top2.shape, I32)
        else:
            better = top2 > gbest
            gidx = jnp.where(better, g, gidx)
            gbest = jnp.where(better, top2, gbest)

    def member(rows, i):
        out = rows[(N_GROUPS - 1) * EXPERTS_PER_GROUP + i]
        for g in range(N_GROUPS - 2, -1, -1):
            out = jnp.where(gidx == g, rows[g * EXPERTS_PER_GROUP + i], out)
        return out

    vb = [member(rb, i) for i in range(EXPERTS_PER_GROUP)]
    vs = [member(rs, i) for i in range(EXPERTS_PER_GROUP)]
    b1, i1 = vb[0], jnp.zeros(gidx.shape, I32)
    for i in range(1, EXPERTS_PER_GROUP):
        better = vb[i] > b1
        i1 = jnp.where(better, i, i1)
        b1 = jnp.where(better, vb[i], b1)
    b2, i2 = jnp.full(b1.shape, -3e38, F32), jnp.zeros(gidx.shape, I32)
    for i in range(EXPERTS_PER_GROUP):
        ok = (i1 != i) & (vb[i] > b2)
        i2 = jnp.where(ok, i, i2)
        b2 = jnp.where(ok, vb[i], b2)

    def pick(rows, idx_):
        out = rows[EXPERTS_PER_GROUP - 1]
        for i in range(EXPERTS_PER_GROUP - 2, -1, -1):
            out = jnp.where(idx_ == i, rows[i], out)
        return out

    w1, w2 = pick(vs, i1), pick(vs, i2)
    den = w1 + w2
    e_o[...] = jnp.concatenate([gidx * EXPERTS_PER_GROUP + i1, gidx * EXPERTS_PER_GROUP + i2], axis=0)
    gw_o[...] = jnp.concatenate([w1 / den, w2 / den], axis=0)


def _outproj(x, o_sets, ln_g, ln_b, w_router, router_bias, tm):
    n = x.shape[0]
    ins = [(x, 'row')]
    for arrs, _ in o_sets:
        ins += [(a, 'row') for a in arrs]
    ins += [(w, 'const') for _, w in o_sets]
    ins += [(ln_g.reshape(1, -1), 'const'), (ln_b.reshape(1, -1), 'const'), (w_router.T, 'const'),
            (router_bias.reshape(-1, 1), 'const')]
    outs = [(D_MODEL, F32, False), (D_MODEL, BF16, False), (TOP_K, I32, True), (TOP_K, F32, True)]
    body = functools.partial(_outproj_body, set_sizes=tuple(len(a) for a, _ in o_sets))
    return _row_call(body, n, tm, ins, outs, 'outproj_ln_router')


def _expert_body(be_ref, nu_ref, x_ref, wg_ref, wu_ref, wd_ref, o_ref, wg_sc, wu_sc, wd_sc):
    i = pl.program_id(0)
    changed = (i == 0) | (be_ref[jnp.maximum(i - 1, 0)] != be_ref[i])

    @pl.when(changed)
    def _():
        wg_sc[...] = wg_ref[...].astype(BF16)
        wu_sc[...] = wu_ref[...].astype(BF16)
        wd_sc[...] = wd_ref[...].astype(BF16)

    @pl.when(i < nu_ref[0])
    def _():
        xb = x_ref[...]
        h = jax.nn.silu(_nn(xb, wg_sc[...])) * _nn(xb, wu_sc[...])
        o_ref[...] = _nn(h.astype(BF16), wd_sc[...])

    @pl.when(i >= nu_ref[0])
    def _():
        o_ref[...] = jnp.zeros_like(o_ref)


def _experts(xg, blk_expert, n_used, w_g, w_u, w_d):
    n_slots = xg.shape[0]
    n_blocks = n_slots // MOE_TILE
    grid_spec = pltpu.PrefetchScalarGridSpec(
        num_scalar_prefetch=2, grid=(n_blocks,),
        in_specs=[pl.BlockSpec((MOE_TILE, D_MODEL), lambda i, be, nu: (i, 0)),
                  pl.BlockSpec((None, D_MODEL, D_EXPERT), lambda i, be, nu: (be[i], 0, 0)),
                  pl.BlockSpec((None, D_MODEL, D_EXPERT), lambda i, be, nu: (be[i], 0, 0)),
                  pl.BlockSpec((None, D_EXPERT, D_MODEL), lambda i, be, nu: (be[i], 0, 0))],
        out_specs=pl.BlockSpec((MOE_TILE, D_MODEL), lambda i, be, nu: (i, 0)),
        scratch_shapes=[pltpu.VMEM((D_MODEL, D_EXPERT), BF16), pltpu.VMEM((D_MODEL, D_EXPERT), BF16),
                        pltpu.VMEM((D_EXPERT, D_MODEL), BF16)])
    return pl.pallas_call(
        _expert_body, out_shape=jax.ShapeDtypeStruct((n_slots, D_MODEL), F32), grid_spec=grid_spec,
        compiler_params=_cparams(("arbitrary",)), name='moe_experts')(blk_expert, n_used, xg, w_g, w_u, w_d)


def _ln2_body(x_ref, y0_ref, y1_ref, gw_ref, g_ref, b_ref, o_ref):
    gw = gw_ref[...]
    y = gw[:, 0:1] * y0_ref[...] + gw[:, 1:2] * y1_ref[...]
    o_ref[...] = _layernorm(ALPHA * x_ref[...] + y, g_ref[...], b_ref[...])


def _moe(x1, x1b, expert_t, gate_t, w_g, w_u, w_d, ln_g, ln_b, tm):
    n = x1.shape[0]
    n_assign = TOP_K * n
    e_flat = expert_t.reshape(-1)
    onehot = (e_flat[:, None] == jnp.arange(N_EXPERTS)[None, :]).astype(I32)
    counts = onehot.sum(0)
    padded = (counts + MOE_TILE - 1) // MOE_TILE * MOE_TILE
    ends = jnp.cumsum(padded)
    starts = ends - padded
    rank = (jnp.cumsum(onehot, 0) * onehot).sum(-1) - 1
    dest = starts[e_flat] + rank
    n_blocks = -(-n_assign // MOE_TILE) + N_EXPERTS
    n_slots = n_blocks * MOE_TILE
    slot_tok = jnp.zeros((n_slots,), I32).at[dest].set(jnp.arange(n_assign, dtype=I32) % n)
    blk_expert = jnp.minimum((jnp.arange(n_blocks)[:, None] * MOE_TILE >= ends[None, :]).sum(-1), N_EXPERTS - 1).astype(I32)
    n_used = (ends[-1] // MOE_TILE).astype(I32).reshape(1)
    xg = jnp.take(x1b, slot_tok, axis=0)
    out = _experts(xg, blk_expert, n_used, w_g, w_u, w_d)
    y = jnp.take(out, dest, axis=0).reshape(TOP_K, n, D_MODEL)
    ins = [(x1, 'row'), (y[0], 'row'), (y[1], 'row'), (gate_t.T, 'row'),
           (ln_g.reshape(1, -1), 'const'), (ln_b.reshape(1, -1), 'const')]
    return _row_call(_ln2_body, n, tm, ins, [(D_MODEL, F32, False)], 'moe_combine_ln')[0]


def _row_tile(n, cap=256):
    tm = cap
    while n % tm:
        tm //= 2
    assert tm >= SUBLANE
    return tm


def _feat_major(a_t, b, t):
    return jnp.transpose(a_t.reshape(a_t.shape[0], b, t), (1, 0, 2))


def _prompt_trunk(x3, mem_prompt, p, slopes):
    b, t, _ = x3.shape
    n = b * t
    x = x3.reshape(n, D_MODEL)
    tm = _row_tile(n)
    pair = lambda g: g // 2
    ckv_rows, kpe_rows, moba_rows, cmp_rows, sel_rows, win_states, mem_kvs = [], [], [], [], [], [], []
    mem2d = mem_prompt.reshape(b * N_MEM, D_MODEL)
    for li in range(DEPTH):
        kind, j = li % 3, li // 3
        w_in = (p['w_in_a'], p['w_in_b'], p['w_in_c'])[kind][j]
        wmem_pad = _pad_head_cols(w_in[:, -MEMQ:], _KV_OF_MEM).astype(BF16)
        w_out = p['w_out'][li]
        wm = p['w_mem_kv'][li]
        mkv, vtm = _inproj(mem2d, [wm.astype(BF16), wm[:, MEMQ:].T.astype(BF16)],
                           [('mm', 0, None, 1.0), ('mmT', 1, None, 1.0)],
                           [(2 * MEMQ, F32, False), (MEMQ, BF16, True)], _row_tile(b * N_MEM), 'mem_kv')
        mem_kvs.append(mkv.reshape(b, N_MEM, 2, N_MEM_HEADS, HEAD_DIM))
        if kind == 0:
            consts = _mla_weights(w_in, p['q_norm_a'][j], p['kv_norm_a'][j], p['w_uq_a'][j], p['w_uk_a'][j],
                                  p['w_uv_a'][j], wmem_pad, True)
            ckv, kpe, q, qmem, kh, vht = _inproj_mla(x, np.arange(t), consts, tm, True, N_MEM_HEADS * LANE)
            ckv_rows.append(ckv.reshape(b, t, KV_LORA))
            kpe_rows.append(kpe.reshape(b, t, QK_ROPE))
            tq, tk = min(512, t), min(256, t)
            qpos = _qpos_table(0, t, tq, 1, tq)
            o_mix = _flash(q.reshape(b, t, -1), kh.reshape(b, t, -1), _feat_major(vht, b, t), qpos, n_groups=1,
                           dq=LANE, dk=LANE, dv=LANE, tq=tq, tk=tk, kblk=lambda g: g, vblk=pair,
                           out_dtype=BF16, name='mla_flash')
            o_sets = [([o_mix.reshape(n, -1)], _pad_head_rows(w_out[:HQ], list(range(N_HEADS))).astype(BF16))]
        elif kind == 1:
            consts = [_pad_head_cols(w_in[:, :HQ], _KV_OF_HEAD).astype(BF16), w_in[:, HQ:HQ + 2 * KVW].astype(BF16),
                      w_in[:, HQ + KVW:HQ + 2 * KVW].T.astype(BF16), wmem_pad]
            plan = [('mm', 0, None, HEAD_DIM ** -0.5), ('prev', 0, None, 1.0), ('mm', 1, None, 1.0),
                    ('mmT', 2, None, 1.0), ('mm', 3, None, HEAD_DIM ** -0.5)]
            outs = [(N_HEADS * LANE, BF16, False), (N_HEADS * LANE, F32, False), (2 * KVW, F32, False),
                    (KVW, BF16, True), (N_MEM_HEADS * LANE, BF16, False)]
            q, qf, kv, vt, qmem = _inproj(x, consts, plan, outs, tm, 'inproj_moba')
            moba_rows.append(kv.reshape(b, t, 2, N_KV_HEADS, HEAD_DIM))
            o_mix = _moba_prompt_attend(q.reshape(b, t, -1), qf.reshape(b, t, -1), kv.reshape(b, t, -1),
                                        _feat_major(vt, b, t), slopes, tq=min(256, t))
            o_sets = [([o_mix.reshape(n, -1)], _pad_head_rows(w_out[:HQ], _KV_OF_HEAD).astype(BF16))]
        else:
            c0 = HQ
            wg_t = jnp.zeros((40, D_MODEL), F32).at[:3 * N_HEADS].set(w_in[:, c0 + 6 * KVW:c0 + 6 * KVW + 3 * N_HEADS].T)
            bg = jnp.zeros((40, 1), F32).at[:3 * N_HEADS, 0].set(p['b_gate_c'][j])
            consts = [_pad_head_cols(w_in[:, :HQ], _KV_OF_HEAD).astype(BF16),
                      w_in[:, c0:c0 + 2 * KVW].astype(BF16), w_in[:, c0 + 2 * KVW:c0 + 4 * KVW].astype(BF16),
                      w_in[:, c0 + 4 * KVW:c0 + 6 * KVW].astype(BF16),
                      w_in[:, c0 + 3 * KVW:c0 + 4 * KVW].T.astype(BF16), w_in[:, c0 + 5 * KVW:c0 + 6 * KVW].T.astype(BF16),
                      wg_t.astype(BF16), bg, wmem_pad]
            plan = [('mm', 0, None, HEAD_DIM ** -0.5), ('mm', 1, None, 1.0), ('mm', 2, None, 1.0), ('mm', 3, None, 1.0),
                    ('mmT', 4, None, 1.0), ('mmT', 5, None, 1.0), ('gateT', 6, 7, 1.0), ('mm', 8, None, HEAD_DIM ** -0.5)]
            outs = [(N_HEADS * LANE, BF16, False), (2 * KVW, F32, False), (2 * KVW, F32, False), (2 * KVW, F32, False),
                    (KVW, BF16, True), (KVW, BF16, True), (40, F32, True), (N_MEM_HEADS * LANE, BF16, False)]
            q, cmp_kv, sel_kv, win_kv, vts, vtw, g_t, qmem = _inproj(x, consts, plan, outs, tm, 'inproj_nsa')
            kvs = (2, N_KV_HEADS, HEAD_DIM)
            cmp_rows.append(cmp_kv.reshape((b, t) + kvs))
            sel_rows.append(sel_kv.reshape((b, t) + kvs))
            keep = min(WINDOW, t)
            win_states.append(win_kv.reshape((b, t) + kvs)[:, t - keep:])
            tq = min(256, t)
            phi = (p['cmp_pe_c'][j], p['phi_w1_c'][j], p['phi_b1_c'][j], p['phi_w2_c'][j], p['phi_b2_c'][j])
            o3 = _nsa_prompt_attend(q.reshape(b, t, -1), cmp_kv.reshape(b, t, -1), sel_kv.reshape(b, t, -1),
                                    win_kv.reshape(b, t, -1), _feat_major(vts, b, t), _feat_major(vtw, b, t),
                                    _gate_lanes(g_t, b, t, tq), slopes, phi, tq=tq)
            o_sets = [([o.reshape(n, -1) for o in o3], _pad_head_rows(w_out[:HQ], _KV_OF_HEAD).astype(BF16))]
        tqm = min(512, t)
        o_mem = _flash(qmem.reshape(b, t, -1), mkv.reshape(b, N_MEM, -1), _feat_major(vtm, b, N_MEM),
                       _qpos_table(0, t, tqm, 1, tqm), n_groups=1, dq=LANE, dk=LANE, dv=LANE, tq=tqm, tk=N_MEM,
                       kblk=pair, vblk=pair, mode='full', out_dtype=BF16, name='mem_flash')
        o_sets.append(([o_mem.reshape(n, -1)], _pad_head_rows(w_out[HQ:], _KV_OF_MEM).astype(BF16)))
        x1, x1b, e_t, g_w = _outproj(x, o_sets, p['ln1_g'][li], p['ln1_b'][li], p['w_router'], p['router_bias'], tm)
        x = _moe(x1, x1b, e_t, g_w, p['w_e_gate'][li], p['w_e_up'][li], p['w_e_down'][li],
                 p['ln2_g'][li], p['ln2_b'][li], tm)
    return (x.reshape(b, t, D_MODEL), jnp.stack(ckv_rows, 1), jnp.stack(kpe_rows, 1), jnp.stack(moba_rows, 1),
            jnp.stack(cmp_rows, 1), jnp.stack(sel_rows, 1), jnp.stack(win_states, 0), jnp.stack(mem_kvs, 0))


_SLOT_S = 4 * HEAD_DIM
PAGES_PER_STEP = 32
CMP_PAGES_PER_STEP = 8


def _decode_body(*refs, cfg):
    (mode, n_groups, dq, ts, n_pp, pw, n_steps, dv, has_new, sel_bs, has_gate, has_slopes, causal, window,
     kpos0, past) = cfg
    refs = list(refs)
    refs.pop(0)
    q_ref = refs.pop(0)
    k_refs = [refs.pop(0) for _ in range(n_pp)]
    k2_refs = [refs.pop(0) for _ in range(n_pp)] if mode == 'mla' else [None] * n_pp
    v_refs = [refs.pop(0) for _ in range(n_pp)] if mode == 'ft' else k_refs
    if has_new:
        knew_ref = refs.pop(0)
        k2new_ref = refs.pop(0) if mode == 'mla' else None
        vnew_ref = refs.pop(0) if mode == 'ft' else knew_ref
    sel_ref = refs.pop(0) if sel_bs else None
    gate_ref = refs.pop(0) if has_gate else None
    qpos_ref = refs.pop(0)
    slope_ref = refs.pop(0) if has_slopes else None
    o_ref, m_sc, l_sc, acc_sc = refs
    rows = n_groups * ts
    step = pl.program_id(1)

    @pl.when(step == 0)
    def _():
        m_sc[...] = jnp.full_like(m_sc, NEG)
        l_sc[...] = jnp.zeros_like(l_sc)
        acc_sc[...] = jnp.zeros_like(acc_sc)

    q = _stack_rows(q_ref, n_groups, dq, rows)
    qpos = qpos_ref[...]

    def scores(k_ref, k2_ref):
        if mode == 'ft':
            return _nn(q, k_ref[...].astype(BF16))
        return (_nt(q[:, :KV_LORA], k_ref[...].astype(BF16))
                + _nn(q[:, KV_LORA:KV_LORA + QK_ROPE], k2_ref[...].astype(BF16)))

    def pv(p, v_ref):
        if mode == 'ft':
            return _nt(p, v_ref[...].astype(BF16))
        return _nn(p, v_ref[...].astype(BF16))

    def update(tiles, vrefs, masked):
        m_prev = m_sc[...]
        mx = tiles[0]
        for t in tiles[1:]:
            mx = jnp.maximum(mx, t)
        m_new = jnp.maximum(m_prev, jnp.max(mx, axis=1, keepdims=True))
        alpha = jnp.exp(m_prev - m_new)
        psum, acc = None, None
        for s, v_ref in zip(tiles, vrefs):
            p = jnp.exp(s - m_new)
            if masked:
                p = jnp.where(s > 0.5 * NEG, p, 0.0)
            psum = p if psum is None else psum + p
            term = pv(p.astype(BF16), v_ref)
            acc = term if acc is None else acc + term
        l_sc[...] = alpha * l_sc[...] + jnp.sum(psum, axis=1, keepdims=True)
        acc_sc[...] = alpha * acc_sc[...] + acc
        m_sc[...] = m_new

    lane = lax.broadcasted_iota(I32, (1, pw), 1)
    tiles = []
    for p in range(n_pp):
        s = scores(k_refs[p], k2_refs[p])
        kpos = kpos0 + (step * n_pp + p) * pw + lane
        d = qpos - kpos
        if has_slopes:
            s = s - slope_ref[...] * d.astype(F32)
        if sel_bs:
            if sel_bs >= pw:
                c = p // (sel_bs // pw)
                s = s + sel_ref[:, c:c + 1]
            else:
                per = pw // sel_bs
                bias = sel_ref[:, p * per:p * per + 1]
                for j in range(1, per):
                    bias = jnp.where(lane >= j * sel_bs, sel_ref[:, p * per + j:p * per + j + 1], bias)
                s = s + bias
        if window is not None:
            s = jnp.where(d <= window, s, NEG)
        tiles.append(s)
    update(tiles, v_refs, masked=bool(sel_bs) or window is not None)

    @pl.when(step == n_steps - 1)
    def _():
        if has_new:
            lane_n = lax.broadcasted_iota(I32, (1, knew_ref.shape[-1] if mode == 'ft' else knew_ref.shape[0]), 1)
            s = scores(knew_ref, k2new_ref)
            d = qpos - (past + lane_n)
            if has_slopes:
                s = s - slope_ref[...] * d.astype(F32)
            s = jnp.where(d >= 0, s, NEG)
            update([s], [vnew_ref], masked=True)
        inv = jnp.where(m_sc[...] > 0.5 * NEG, 1.0 / jnp.maximum(l_sc[...], TINY), 0.0)
        if has_gate:
            inv = inv * gate_ref[...]
        o_ref[...] = _unstack_rows(acc_sc[...] * inv, n_groups, ts).astype(o_ref.dtype)


def _decode(q, page_table, k_pages, v_pages, *, mode, j, n_groups, dq, dv, n_pp, pw, n_steps, paged, k2_pages=None,
            new=None, sel=None, sel_bs=None, gates=None, qpos=None, slopes=None, window=None, kpos0=0, past=0,
            out_dtype=BF16, name='decode'):
    db, ts, _ = q.shape
    rows = n_groups * ts
    in_specs = [pl.BlockSpec((None, ts, n_groups * dq), lambda b, s, pt: (b, 0, 0))]
    args = [q]

    def add_pages(arr, c):
        for pg in range(n_pp):
            if paged:
                if c is None:
                    spec = pl.BlockSpec((None, None) + arr.shape[2:],
                                        lambda b, s, pt, pg=pg: (pt[b * (n_pp * n_steps) + s * n_pp + pg], j, 0, 0))
                else:
                    spec = pl.BlockSpec((None, None, None) + arr.shape[3:],
                                        lambda b, s, pt, pg=pg, c=c: (pt[b * (n_pp * n_steps) + s * n_pp + pg], j, c, 0, 0))
            else:
                spec = pl.BlockSpec((None, None, None, arr.shape[3], pw),
                                    lambda b, s, pt, pg=pg, c=c: (j, b, c, 0, s * n_pp + pg))
            in_specs.append(spec)
            args.append(arr)

    if mode == 'ft':
        add_pages(k_pages, 0)
        add_pages(v_pages, 1)
    else:
        add_pages(k_pages, None)
        add_pages(k2_pages, None)
    if new is not None:
        for a in new:
            in_specs.append(pl.BlockSpec((None,) + a.shape[1:], lambda b, s, pt, nd=a.ndim: (b,) + (0,) * (nd - 1)))
            args.append(a)
    if sel is not None:
        in_specs.append(pl.BlockSpec((None, None, rows, sel.shape[-1]), lambda b, s, pt: (b, s, 0, 0)))
        args.append(sel)
    if gates is not None:
        in_specs.append(pl.BlockSpec((None, rows, 1), lambda b, s, pt: (b, 0, 0)))
        args.append(gates)
    in_specs.append(pl.BlockSpec((rows, 1), lambda b, s, pt: (0, 0)))
    args.append(qpos)
    if slopes is not None:
        in_specs.append(pl.BlockSpec((rows, 1), lambda b, s, pt: (0, 0)))
        args.append(slopes)
    cfg = (mode, n_groups, dq, ts, n_pp, pw, n_steps, dv, new is not None, sel_bs if sel is not None else None,
           gates is not None, slopes is not None, True, window, kpos0, past)
    grid_spec = pltpu.PrefetchScalarGridSpec(
        num_scalar_prefetch=1, grid=(db, n_steps), in_specs=in_specs,
        out_specs=pl.BlockSpec((None, ts, n_groups * dv), lambda b, s, pt: (b, 0, 0)),
        scratch_shapes=[pltpu.VMEM((rows, 1), F32), pltpu.VMEM((rows, 1), F32), pltpu.VMEM((rows, dv), F32)])
    return pl.pallas_call(
        functools.partial(_decode_body, cfg=cfg),
        out_shape=jax.ShapeDtypeStruct((db, ts, n_groups * dv), out_dtype), grid_spec=grid_spec,
        compiler_params=_cparams(("parallel", "arbitrary")), name=name)(page_table.reshape(-1), *args)


def _kmean_pages_body(pt_ref, *refs, n_pp):
    k_refs = refs[:n_pp]
    o_ref = refs[n_pp]
    step = pl.program_id(1)

    @pl.when(step == 0)
    def _():
        o_ref[...] = jnp.zeros_like(o_ref)

    lane = lax.broadcasted_iota(I32, (1, o_ref.shape[-1]), 1)
    per = MOBA_BLOCK // PAGE_SIZE
    acc = o_ref[...]
    for i in range(n_pp // per):
        tot = k_refs[per * i][...]
        for r in range(1, per):
            tot = tot + k_refs[per * i + r][...]
        mean = jnp.sum(tot, axis=1, keepdims=True) * (1.0 / MOBA_BLOCK)
        acc = jnp.where(lane == step * (n_pp // per) + i, mean, acc)
    o_ref[...] = acc


def _kmean_pages(k_pages, page_table, j, n_pp, n_steps):
    db = page_table.shape[0]
    f = k_pages.shape[3]
    in_specs = [pl.BlockSpec((None, None, None, f, PAGE_SIZE),
                             lambda b, s, pt, pg=pg: (pt[b * (n_pp * n_steps) + s * n_pp + pg], j, 0, 0, 0))
                for pg in range(n_pp)]
    grid_spec = pltpu.PrefetchScalarGridSpec(
        num_scalar_prefetch=1, grid=(db, n_steps), in_specs=in_specs,
        out_specs=pl.BlockSpec((None, f, LANE), lambda b, s, pt: (b, 0, 0)))
    return pl.pallas_call(
        functools.partial(_kmean_pages_body, n_pp=n_pp), out_shape=jax.ShapeDtypeStruct((db, f, LANE), F32),
        grid_spec=grid_spec, compiler_params=_cparams(("parallel", "arbitrary")), name='moba_kmean_pages')(
            page_table.reshape(-1), *([k_pages] * n_pp))


def _gate_rows_body(q_ref, kmt_ref, qpos_ref, o_ref, *, n_groups, dq, ts, n_cand):
    q = _stack_rows(q_ref, n_groups, dq, n_groups * ts)
    gate = _nn(q, kmt_ref[...], precision=HIGHEST)
    own = qpos_ref[...] // MOBA_BLOCK
    lane = lax.broadcasted_iota(I32, gate.shape, 1)
    elig = lane < own
    g = jnp.where(elig, gate, NEG)
    rank = jnp.zeros(g.shape, I32)
    for i in range(n_cand):
        gi = g[:, i:i + 1]
        rank = rank + ((gi > g) | ((gi == g) & (i < lane))).astype(I32)
    selected = (elig & (rank < MOBA_TOPK)) | (lane == own)
    o_ref[...] = jnp.where(selected, 0.0, NEG)


def _gate_rows(qf, kmean_t, qpos, *, n_groups, dq, n_cand):
    db, ts, _ = qf.shape
    rows = n_groups * ts
    return pl.pallas_call(
        functools.partial(_gate_rows_body, n_groups=n_groups, dq=dq, ts=ts, n_cand=n_cand),
        out_shape=jax.ShapeDtypeStruct((db, rows, LANE), F32), grid=(db,),
        in_specs=[pl.BlockSpec((None, ts, n_groups * dq), lambda b: (b, 0, 0)),
                  pl.BlockSpec((None, dq, LANE), lambda b: (b, 0, 0)),
                  pl.BlockSpec((rows, 1), lambda b: (0, 0))],
        out_specs=pl.BlockSpec((None, rows, LANE), lambda b: (b, 0, 0)),
        compiler_params=_cparams(("parallel",)), name='moba_gate_rows')(qf, kmean_t, qpos)


def _cmp_stage1_body(pt_ref, *refs, n_pp, n_steps, n_sub, ts):
    page_refs = refs[:n_pp]
    new_ref, perm_ref, w_ref, o_ref, x_sc = refs[n_pp:]
    step = pl.program_id(1)
    per = PAGE_SIZE // CMP_STRIDE
    perm = perm_ref[...]
    for pg in range(n_pp):
        n0 = pl.multiple_of((step * n_pp + pg) * per, per)
        for c in range(2):
            rows = _nn(page_refs[pg][c].astype(BF16), perm).T
            for p in range(CMP_STRIDE):
                x_sc[c, p, pl.ds(n0, per), :] = rows[p * per:(p + 1) * per, :]

    @pl.when(step == n_steps - 1)
    def _():
        n_past = n_pp * n_steps * per
        for c in range(2):
            x_sc[c, :, pl.ds(n_past, n_sub - n_past), :] = jnp.zeros((CMP_STRIDE, n_sub - n_past, KVW), F32)
            for t in range(ts):
                x_sc[c, t, pl.ds(n_past, 1), :] = new_ref[t:t + 1, c * KVW:(c + 1) * KVW]
            acc = None
            for p in range(CMP_STRIDE):
                term = _nn(x_sc[c, p].astype(BF16), w_ref[c, p])
                acc = term if acc is None else acc + term
            o_ref[:, c * 2 * KVW:(c + 1) * 2 * KVW] = acc


def _cmp_stage1(cmp_pages, page_table, j, cmp_new, phi_w1, *, n_pp, n_steps, n_sub):
    db, ts, _ = cmp_new.shape
    w1r = phi_w1.reshape(2, 2, CMP_STRIDE, HEAD_DIM, CMP_HIDDEN)
    w = jnp.einsum('chpdx,kl->cpkdhlx', w1r, jnp.eye(N_KV_HEADS, dtype=F32))
    w = w.reshape(2, CMP_STRIDE, KVW, 2 * KVW).astype(BF16)
    assert ts <= CMP_STRIDE and (n_pp * n_steps * PAGE_SIZE) % CMP_STRIDE == 0
    per = PAGE_SIZE // CMP_STRIDE
    r = np.arange(PAGE_SIZE)
    perm = np.zeros((PAGE_SIZE, PAGE_SIZE), np.float32)
    perm[r, (r % CMP_STRIDE) * per + r // CMP_STRIDE] = 1.0
    perm = jnp.asarray(perm, dtype=BF16)
    in_specs = [pl.BlockSpec((None, None) + cmp_pages.shape[2:],
                             lambda b, s, pt, pg=pg: (pt[b * (n_pp * n_steps) + s * n_pp + pg], j, 0, 0, 0))
                for pg in range(n_pp)]
    in_specs += [pl.BlockSpec((None, ts, 2 * KVW), lambda b, s, pt: (b, 0, 0)),
                 pl.BlockSpec(perm.shape, lambda b, s, pt: (0, 0)),
                 pl.BlockSpec(w.shape, lambda b, s, pt: (0, 0, 0, 0))]
    grid_spec = pltpu.PrefetchScalarGridSpec(
        num_scalar_prefetch=1, grid=(db, n_steps), in_specs=in_specs,
        out_specs=pl.BlockSpec((None, n_sub, 4 * KVW), lambda b, s, pt: (b, 0, 0)),
        scratch_shapes=[pltpu.VMEM((2, CMP_STRIDE, n_sub, KVW), F32)])
    return pl.pallas_call(
        functools.partial(_cmp_stage1_body, n_pp=n_pp, n_steps=n_steps, n_sub=n_sub, ts=ts),
        out_shape=jax.ShapeDtypeStruct((db, n_sub, 4 * KVW), F32), grid_spec=grid_spec,
        compiler_params=_cparams(("parallel", "arbitrary")), name='nsa_compress_pages')(
            page_table.reshape(-1), *([cmp_pages] * n_pp), cmp_new, perm, w)


def _sample_lane_gates(g_t, db, ts, rp):
    g = g_t[:3 * N_HEADS].reshape(3, N_HEADS, db, ts)
    g = jnp.transpose(g, (0, 2, 1, 3)).reshape(3, db, N_HEADS * ts)
    return jnp.pad(g, ((0, 0), (0, 0), (0, rp - N_HEADS * ts))).reshape(3, db, 1, 1, 1, rp)


def _sample_trunk(x3, p, caches, page_table, slopes):
    db, ts, _ = x3.shape
    n = db * ts
    past = page_table.shape[1] * PAGE_SIZE
    x = x3.reshape(n, D_MODEL)
    tm = _row_tile(n)
    rp = LANE
    assert N_HEADS * ts <= rp and ts % SUBLANE == 0
    zero = lambda g: 0
    qpos = _qpos_table(past, ts, ts, N_HEADS, rp)
    slope_tab = _lane_table(slopes.reshape(1, N_HEADS), ts, rp)
    ckv_rows, kpe_rows, moba_rows, cmp_rows, sel_rows, win_states = [], [], [], [], [], []
    pos_rows = past + np.arange(max(tm, ts)) % ts
    kvs = (2, N_KV_HEADS, HEAD_DIM)
    n_pages = page_table.shape[1]
    n_pp = PAGES_PER_STEP
    assert n_pages % n_pp == 0 and past % MOBA_BLOCK == 0
    n_steps = n_pages // n_pp
    rows = N_HEADS * ts
    qpos_col = jnp.asarray((past + np.arange(rows) % ts).astype(np.int32).reshape(rows, 1))
    qpos_mem_col = qpos_col[:N_MEM_HEADS * ts]
    slope_col = jnp.asarray(np.repeat(slopes, ts).astype(np.float32).reshape(rows, 1))

    def feature_major(c):
        nd = c.ndim
        c = jnp.transpose(c, tuple(range(nd - 4)) + (nd - 3, nd - 2, nd - 1, nd - 4))
        return c.reshape(c.shape[:nd - 4] + (2, KVW, c.shape[-1]))

    def new_tiles(kv_new):
        t = jnp.transpose(kv_new.reshape(db, ts, 2, KVW), (0, 2, 3, 1))
        t = jnp.pad(t, ((0, 0), (0, 0), (0, 0), (0, PAGE_SIZE - ts)))
        return t[:, 0], t[:, 1]

    def row_gates(g_t):
        g = g_t[:3 * N_HEADS].reshape(3, N_HEADS, db, ts)
        return jnp.transpose(g, (0, 2, 1, 3)).reshape(3, db, rows, 1)

    def step_sel(sel_rows_, n_cols):
        s = sel_rows_[:, :, :n_steps * n_cols].reshape(db, rows, n_steps, n_cols)
        return jnp.transpose(s, (0, 2, 1, 3))

    moba_pages = feature_major(caches['moba_kv'])
    cmp_pages = feature_major(caches['nsa_cmp'])
    sel_pages = feature_major(caches['nsa_sel'])
    win_ft = feature_major(caches['nsa_win'])
    mem_ft = feature_major(caches['mem_kv'])
    kpe_pages = jnp.swapaxes(caches['mla_kpe'], 2, 3)
    for li in range(DEPTH):
        kind, j = li % 3, li // 3
        w_in = (p['w_in_a'], p['w_in_b'], p['w_in_c'])[kind][j]
        wmem_pad = _pad_head_cols(w_in[:, -MEMQ:], _KV_OF_MEM, _SLOT_S).astype(BF16)
        w_out = p['w_out'][li]
        if kind == 0:
            consts = _mla_weights(w_in, p['q_norm_a'][j], p['kv_norm_a'][j], p['w_uq_a'][j], p['w_uk_a'][j],
                                  p['w_uv_a'][j], wmem_pad, False)
            ckv, kpe, q, qmem = _inproj_mla(x, pos_rows, consts, tm, False, N_MEM_HEADS * _SLOT_S)
            ckv_rows.append(ckv.reshape(db, ts, KV_LORA))
            kpe_rows.append(kpe.reshape(db, ts, QK_ROPE))
            slot_a = KV_LORA + LANE
            w_abs = jnp.zeros((N_HEADS, LANE, N_HEADS, slot_a), F32)
            eye = jnp.eye(QK_ROPE, dtype=F32)
            for h in range(N_HEADS):
                w_abs = w_abs.at[h, :QK_NOPE, h, :KV_LORA].set(p['w_uk_a'][j][:, h, :].T)
                w_abs = w_abs.at[h, QK_NOPE:QK_NOPE + QK_ROPE, h, KV_LORA:KV_LORA + QK_ROPE].set(eye)
            q_abs = _mm(q, w_abs.reshape(N_HEADS * LANE, N_HEADS * slot_a).astype(BF16), tm=tm, tn=slot_a,
                        tk=N_HEADS * LANE, out_dtype=BF16, name='mla_absorb_q')
            ckv_new = jnp.pad(ckv.reshape(db, ts, KV_LORA), ((0, 0), (0, PAGE_SIZE - ts), (0, 0)))
            kpe_new = jnp.pad(jnp.swapaxes(kpe.reshape(db, ts, QK_ROPE), 1, 2), ((0, 0), (0, 0), (0, PAGE_SIZE - ts)))
            o_lat = _decode(q_abs.reshape(db, ts, -1), page_table, caches['mla_ckv'], None, mode='mla', j=j,
                            n_groups=N_HEADS, dq=slot_a, dv=KV_LORA, n_pp=n_pp, pw=PAGE_SIZE, n_steps=n_steps,
                            paged=True, k2_pages=kpe_pages, new=(ckv_new, kpe_new), qpos=qpos_col, past=past,
                            out_dtype=BF16, name='mla_decode')
            w_uv_bd = jnp.einsum('chv,hg->hcgv', p['w_uv_a'][j], jnp.eye(N_HEADS, dtype=F32)).reshape(
                N_HEADS * KV_LORA, HQ)
            o_mix = _mm(o_lat.reshape(n, -1), w_uv_bd.astype(BF16), tm=tm, tn=HQ, tk=N_HEADS * KV_LORA,
                        out_dtype=BF16, name='mla_value_up')
            o_sets = [([o_mix], w_out[:HQ].astype(BF16))]
        elif kind == 1:
            consts = [_pad_head_cols(w_in[:, :HQ], _KV_OF_HEAD, _SLOT_S).astype(BF16),
                      w_in[:, HQ:HQ + 2 * KVW].astype(BF16), wmem_pad]
            plan = [('mm', 0, None, HEAD_DIM ** -0.5), ('prev', 0, None, 1.0), ('mm', 1, None, 1.0),
                    ('mm', 2, None, HEAD_DIM ** -0.5)]
            outs = [(N_HEADS * _SLOT_S, BF16, False), (N_HEADS * _SLOT_S, F32, False), (2 * KVW, F32, False),
                    (N_MEM_HEADS * _SLOT_S, BF16, False)]
            q, qf, kv, qmem = _inproj(x, consts, plan, outs, tm, 'inproj_moba_s')
            moba_rows.append(kv.reshape((db, ts) + kvs))
            n_mean = past // MOBA_BLOCK
            assert n_mean <= LANE
            kmean_t = _kmean_pages(moba_pages, page_table, j, n_pp, n_steps)
            selb = _gate_rows(qf.reshape(db, ts, -1), kmean_t, qpos_col, n_groups=N_HEADS, dq=_SLOT_S, n_cand=n_mean)
            o_mix = _decode(q.reshape(db, ts, -1), page_table, moba_pages, moba_pages, mode='ft', j=j,
                            n_groups=N_HEADS, dq=_SLOT_S, dv=_SLOT_S, n_pp=n_pp, pw=PAGE_SIZE, n_steps=n_steps,
                            paged=True, new=new_tiles(kv), sel=step_sel(selb, n_pp * PAGE_SIZE // MOBA_BLOCK),
                            sel_bs=MOBA_BLOCK, qpos=qpos_col, slopes=slope_col, past=past, out_dtype=BF16,
                            name='moba_decode')
            o_sets = [([o_mix.reshape(n, -1)], _pad_head_rows(w_out[:HQ], _KV_OF_HEAD, _SLOT_S).astype(BF16))]
        else:
            c0 = HQ
            wg_t = jnp.zeros((40, D_MODEL), F32).at[:3 * N_HEADS].set(w_in[:, c0 + 6 * KVW:c0 + 6 * KVW + 3 * N_HEADS].T)
            bg = jnp.zeros((40, 1), F32).at[:3 * N_HEADS, 0].set(p['b_gate_c'][j])
            consts = [_pad_head_cols(w_in[:, :HQ], _KV_OF_HEAD, _SLOT_S).astype(BF16),
                      w_in[:, c0:c0 + 2 * KVW].astype(BF16), w_in[:, c0 + 2 * KVW:c0 + 4 * KVW].astype(BF16),
                      w_in[:, c0 + 4 * KVW:c0 + 6 * KVW].astype(BF16), wg_t.astype(BF16), bg, wmem_pad]
            plan = [('mm', 0, None, HEAD_DIM ** -0.5), ('mm', 1, None, 1.0), ('mm', 2, None, 1.0), ('mm', 3, None, 1.0),
                    ('gateT', 4, 5, 1.0), ('mm', 6, None, HEAD_DIM ** -0.5)]
            outs = [(N_HEADS * _SLOT_S, BF16, False), (2 * KVW, F32, False), (2 * KVW, F32, False),
                    (2 * KVW, F32, False), (40, F32, True), (N_MEM_HEADS * _SLOT_S, BF16, False)]
            q, cmp_kv, sel_kv, win_kv, g_t, qmem = _inproj(x, consts, plan, outs, tm, 'inproj_nsa_s')
            cmp_rows.append(cmp_kv.reshape((db, ts) + kvs))
            sel_rows.append(sel_kv.reshape((db, ts) + kvs))
            wlen = win_ft.shape[-1]
            keep = min(WINDOW, past + ts)
            wk_new, wv_new = new_tiles(win_kv)
            win_seq_ft = jnp.concatenate([win_ft[j], jnp.stack([wk_new, wv_new], 1)[..., :ts]], -1)[..., wlen + ts - keep:]
            win_states.append(jnp.transpose(win_seq_ft.reshape((db,) + kvs + (keep,)), (0, 4, 1, 2, 3)))
            gates = _sample_lane_gates(g_t, db, ts, rp)
            gate_rows = row_gates(g_t)
            q3 = q.reshape(db, ts, -1)
            length = past + ts
            n_cmp = -(-length // CMP_STRIDE) - 1
            n_sub = _round_up(n_cmp + 1, SUBLANE)
            phi = (p['cmp_pe_c'][j], p['phi_w1_c'][j], p['phi_b1_c'][j], p['phi_w2_c'][j], p['phi_b2_c'][j])
            fs = _cmp_stage1(cmp_pages, page_table, j, cmp_kv.reshape(db, ts, -1), phi[1],
                             n_pp=CMP_PAGES_PER_STEP, n_steps=n_pages // CMP_PAGES_PER_STEP, n_sub=n_sub)
            kc, vct = _compress_stage2(fs, *phi)
            n_slc = -(-length // SEL_BLOCK)
            group_of_lane = np.full((rp,), -1, np.int64)
            lanes = np.arange(N_HEADS * ts)
            group_of_lane[:N_HEADS * ts] = (lanes // ts // GQA) * ts + lanes % ts
            o_cmp, sel = _nsa_cmp(q3, kc, vct, qpos, slope_tab, gates[0], n_groups=N_HEADS, dq=_SLOT_S, dk=_SLOT_S,
                                  dv=_SLOT_S, tq=ts, kblk=zero, n_cmp=n_cmp, n_slc=n_slc, group_of_lane=group_of_lane)
            assert past % SEL_BLOCK == 0
            sel_by_row = jnp.swapaxes(sel[:, 0, 0], 1, 2)[:, :rows, :]
            o_sel = _decode(q3, page_table, sel_pages, sel_pages, mode='ft', j=j, n_groups=N_HEADS, dq=_SLOT_S,
                            dv=_SLOT_S, n_pp=n_pp, pw=PAGE_SIZE, n_steps=n_steps, paged=True, new=new_tiles(sel_kv),
                            sel=step_sel(sel_by_row, n_pp * PAGE_SIZE // SEL_BLOCK), sel_bs=SEL_BLOCK,
                            gates=gate_rows[1], qpos=qpos_col, slopes=slope_col, past=past, out_dtype=F32,
                            name='nsa_sel_decode')
            o_win = _decode(q3, page_table, win_ft, win_ft, mode='ft', j=j, n_groups=N_HEADS, dq=_SLOT_S, dv=_SLOT_S,
                            n_pp=1, pw=wlen, n_steps=1, paged=False, new=(wk_new, wv_new), gates=gate_rows[2],
                            qpos=qpos_col, slopes=slope_col, window=WINDOW, kpos0=past - wlen, past=past,
                            out_dtype=F32, name='nsa_win_decode')
            o_sets = [([o.reshape(n, -1) for o in (o_cmp, o_sel, o_win)],
                       _pad_head_rows(w_out[:HQ], _KV_OF_HEAD, _SLOT_S).astype(BF16))]
        o_mem = _decode(qmem.reshape(db, ts, -1), page_table, mem_ft, mem_ft, mode='ft', j=li, n_groups=N_MEM_HEADS,
                        dq=_SLOT_S, dv=_SLOT_S, n_pp=1, pw=N_MEM, n_steps=1, paged=False, qpos=qpos_mem_col,
                        out_dtype=BF16, name='mem_decode')
        o_sets.append(([o_mem.reshape(n, -1)], _pad_head_rows(w_out[HQ:], _KV_OF_MEM, _SLOT_S).astype(BF16)))
        x1, x1b, e_t, g_w = _outproj(x, o_sets, p['ln1_g'][li], p['ln1_b'][li], p['w_router'], p['router_bias'], tm)
        x = _moe(x1, x1b, e_t, g_w, p['w_e_gate'][li], p['w_e_up'][li], p['w_e_down'][li],
                 p['ln2_g'][li], p['ln2_b'][li], tm)
    return (x.reshape(db, ts, D_MODEL), jnp.stack(ckv_rows, 1), jnp.stack(kpe_rows, 1), jnp.stack(moba_rows, 1),
            jnp.stack(cmp_rows, 1), jnp.stack(sel_rows, 1), jnp.stack(win_states, 0))


def kernel(x_prompt, x_sample, cache_mla_ckv, cache_mla_kpe, cache_moba_kv, cache_nsa_cmp_kv, cache_nsa_sel_kv,
           state_nsa_win_kv, cache_mem_kv, page_table, mem_prompt, w_in_a, q_norm_a, kv_norm_a, w_uq_a, w_uk_a,
           w_uv_a, w_in_b, w_in_c, b_gate_c, cmp_pe_c, phi_w1_c, phi_b1_c, phi_w2_c, phi_b2_c, w_mem_kv, w_out,
           ln1_g, ln1_b, ln2_g, ln2_b, w_router, router_bias, w_e_gate, w_e_up, w_e_down):
    p = dict(w_in_a=w_in_a, q_norm_a=q_norm_a, kv_norm_a=kv_norm_a, w_uq_a=w_uq_a, w_uk_a=w_uk_a, w_uv_a=w_uv_a,
             w_in_b=w_in_b, w_in_c=w_in_c, b_gate_c=b_gate_c, cmp_pe_c=cmp_pe_c, phi_w1_c=phi_w1_c,
             phi_b1_c=phi_b1_c, phi_w2_c=phi_w2_c, phi_b2_c=phi_b2_c, w_mem_kv=w_mem_kv, w_out=w_out, ln1_g=ln1_g,
             ln1_b=ln1_b, ln2_g=ln2_g, ln2_b=ln2_b, w_router=w_router, router_bias=router_bias, w_e_gate=w_e_gate,
             w_e_up=w_e_up, w_e_down=w_e_down)
    caches = dict(mla_ckv=cache_mla_ckv, mla_kpe=cache_mla_kpe, moba_kv=cache_moba_kv, nsa_cmp=cache_nsa_cmp_kv,
                  nsa_sel=cache_nsa_sel_kv, nsa_win=state_nsa_win_kv, mem_kv=cache_mem_kv)
    slopes = _alibi_slopes(N_HEADS)
    y_p, p_ckv, p_kpe, p_moba, p_cmp, p_sel, p_win, p_mem = _prompt_trunk(x_prompt, mem_prompt, p, slopes)
    y_s, s_ckv, s_kpe, s_moba, s_cmp, s_sel, s_win = _sample_trunk(x_sample, p, caches, page_table, slopes)
    return (y_p, y_s, p_ckv, p_kpe, p_moba, p_cmp, p_sel, p_win, p_mem,
            s_ckv, s_kpe, s_moba, s_cmp, s_sel, s_win)
```
